```python
import jax, jax.numpy as jnp
from jax import lax
import numpy as np

D_MODEL = 1024
BATCH = 4
SEQ = 8192
DEPTH = 1
DEC_BATCH = 128
DEC_SEQ = 8
PAST_LEN = 16384
PAGE_SIZE = 128

HEAD_DIM = 64
N_RWKV_HEADS = 8
RWKV_WIDTH = N_RWKV_HEADS * HEAD_DIM
N_Q_HEADS = 8
N_KV_HEADS = 2
GROUP = N_Q_HEADS // N_KV_HEADS
ATT_WIDTH = N_Q_HEADS * HEAD_DIM
KV_WIDTH = N_KV_HEADS * HEAD_DIM
MIX_WIDTH = RWKV_WIDTH + ATT_WIDTH
DECAY_LORA = 64
AAA_LORA = 64
GATE_LORA = 128
RWKV_COLS = 3 * RWKV_WIDTH + DECAY_LORA + AAA_LORA + GATE_LORA
IN_COLS = RWKV_COLS + ATT_WIDTH + 2 * KV_WIDTH
RWKV_SPLITS = (RWKV_WIDTH, 2 * RWKV_WIDTH, 3 * RWKV_WIDTH,
               3 * RWKV_WIDTH + DECAY_LORA, 3 * RWKV_WIDTH + DECAY_LORA + AAA_LORA)
WINDOW = 128
ROPE_THETA = 500000.0
ROPE_DIM = HEAD_DIM // 4
D_FF = 2816
N_ADA = 9
NORM_EPS = 1e-5
LNX_EPS = 64e-5
NEG_INF = -1e30

kernel_name = 'hymba_rwkv7_swa_sink_macaron_adaln_step'


def rms_norm(x, g):
    xf = x.astype(jnp.float32)
    y = xf * lax.rsqrt(jnp.mean(xf * xf, axis=-1, keepdims=True) + NORM_EPS)
    return (y * g.astype(jnp.float32)).astype(x.dtype)


def modulate(x, g, shift, scale):
    return rms_norm(x, g) * (1 + scale) + shift


def swiglu(h, w_gu, w_down):
    gate, up = jnp.split(h @ w_gu, 2, axis=-1)
    return (jax.nn.silu(gate) * up) @ w_down


def partial_rope(x, pos):
    inv_freq = ROPE_THETA ** (-jnp.arange(0, ROPE_DIM, 2, dtype=jnp.float32) / ROPE_DIM)
    ang = pos.astype(jnp.float32)[:, None] * inv_freq[None, :]
    cos, sin = jnp.cos(ang)[:, None, :], jnp.sin(ang)[:, None, :]
    xr = x[..., :ROPE_DIM].astype(jnp.float32)
    x1, x2 = xr[..., :ROPE_DIM // 2], xr[..., ROPE_DIM // 2:]
    rot = jnp.concatenate([x1 * cos - x2 * sin, x2 * cos + x1 * sin], axis=-1)
    return jnp.concatenate([rot.astype(x.dtype), x[..., ROPE_DIM:]], axis=-1)


def window_mask(q_pos, k_pos):
    qp, kp = q_pos[..., :, None], k_pos[..., None, :]
    return (kp <= qp) & (qp - kp < WINDOW) & (kp >= 0)


def sink_attention(q, k, v, mask, sinks):
    s = jnp.einsum('...qhgd,...khd->...hgqk', q.astype(jnp.float32), k.astype(jnp.float32)) * (HEAD_DIM ** -0.5)
    s = jnp.where(mask, s, NEG_INF)
    sink = sinks.astype(jnp.float32).reshape(N_KV_HEADS, GROUP, 1, 1)
    m = jnp.maximum(jnp.max(s, axis=-1, keepdims=True), sink)
    p = jnp.exp(s - m)
    p = p / (jnp.sum(p, axis=-1, keepdims=True) + jnp.exp(sink - m))
    o = jnp.einsum('...hgqk,...khd->...qhgd', p, v.astype(jnp.float32))
    return o.astype(v.dtype)


def swa_prompt(q, k, v, sinks):
    B, S = q.shape[:2]
    nb = S // WINDOW
    qb = q.reshape(B, nb, WINDOW, N_KV_HEADS, GROUP, HEAD_DIM)
    kb = k.reshape(B, nb, WINDOW, N_KV_HEADS, HEAD_DIM)
    vb = v.reshape(B, nb, WINDOW, N_KV_HEADS, HEAD_DIM)

    def with_prev(t):
        prev = jnp.pad(t[:, :-1], ((0, 0), (1, 0), (0, 0), (0, 0), (0, 0)))
        return jnp.concatenate([prev, t], axis=2)

    blk = jnp.arange(nb)[:, None] * WINDOW
    q_pos = blk + jnp.arange(WINDOW)[None, :]
    k_pos = blk - WINDOW + jnp.arange(2 * WINDOW)[None, :]
    mask = window_mask(q_pos, k_pos)[None, :, None, None]
    o = sink_attention(qb, with_prev(kb), with_prev(vb), mask, sinks)
    return o.reshape(B, S, ATT_WIDTH), k[:, -WINDOW:], v[:, -WINDOW:]


def swa_sample(q, k, v, k_buf, v_buf, pos0, sinks):
    B, T = q.shape[:2]
    kk = jnp.concatenate([k_buf.astype(k.dtype), k], axis=1)
    vv = jnp.concatenate([v_buf.astype(v.dtype), v], axis=1)
    q_pos = pos0 + jnp.arange(T)
    k_pos = jnp.concatenate([pos0 - WINDOW + jnp.arange(WINDOW), q_pos])
    mask = window_mask(q_pos, k_pos)[None, None, None]
    o = sink_attention(q.reshape(B, T, N_KV_HEADS, GROUP, HEAD_DIM), kk, vv, mask, sinks)
    return o.reshape(B, T, ATT_WIDTH), kk[:, -WINDOW:], vv[:, -WINDOW:]


def wkv7_step(S, inp):
    r, d, k, v, a, b = inp
    sa = jnp.einsum('bhij,bhj->bhi', S, a)
    S = S * d[:, :, None, :] + sa[..., None] * b[:, :, None, :] + v[..., None] * k[:, :, None, :]
    return S, jnp.einsum('bhij,bhj->bhi', S, r)


def rwkv7_heads(u, u_prev, S0, lw):
    f32 = jnp.float32
    B, T, _ = u.shape
    u_shift = jnp.concatenate([u_prev[:, None, :].astype(u.dtype), u[:, :-1]], axis=1)
    z = u + (u_shift - u) * lw['mu_shift']
    r, k, v, zw, za, zg = jnp.split(z, RWKV_SPLITS, axis=-1)
    w_log = -jax.nn.softplus(-(lw['rwkv_w0'] + jnp.tanh(zw) @ lw['rwkv_w2'])) - 0.5
    decay = jnp.exp(-jnp.exp(w_log.astype(f32)))
    a = jax.nn.sigmoid(lw['rwkv_a0'] + za @ lw['rwkv_a2'])
    g = jax.nn.sigmoid(zg) @ lw['rwkv_g2']
    heads = lambda t: t.astype(f32).reshape(B, T, N_RWKV_HEADS, HEAD_DIM)
    kk = heads(k * lw['rwkv_k_k'])
    kk = kk * lax.rsqrt(jnp.maximum(jnp.sum(kk * kk, axis=-1, keepdims=True), 1e-24))
    k = k * (1 + (a - 1) * lw['rwkv_k_a'])
    rh, kh, vh, ah, dh = heads(r), heads(k), heads(v), heads(a), heads(decay)
    seq = tuple(t.transpose(1, 0, 2, 3) for t in (rh, dh, kh, vh, -kk, kk * ah))
    S_new, o = lax.scan(wkv7_step, S0.astype(f32), seq)
    o = o.transpose(1, 0, 2, 3)
    mu = jnp.mean(o, axis=-1, keepdims=True)
    var = jnp.mean(jnp.square(o - mu), axis=-1, keepdims=True)
    o = ((o - mu) * lax.rsqrt(var + LNX_EPS)).reshape(B, T, RWKV_WIDTH)
    o = o * lw['ln_x_w'].astype(f32) + lw['ln_x_b'].astype(f32)
    bonus = jnp.sum(rh * kh * lw['rwkv_r_k'].astype(f32), axis=-1, keepdims=True) * vh
    o = (o + bonus.reshape(B, T, RWKV_WIDTH)) * g.astype(f32)
    return o.astype(u.dtype), S_new


def parallel_heads(h, u_prev, S0, k_buf, v_buf, pos0, lw):
    B, T, _ = h.shape
    proj = h @ lw['w_in']
    u = proj[..., :RWKV_COLS]
    q, k, v = jnp.split(proj[..., RWKV_COLS:], (ATT_WIDTH, ATT_WIDTH + KV_WIDTH), axis=-1)
    rwkv_out, S_new = rwkv7_heads(u, u_prev, S0, lw)
    pos = pos0 + jnp.arange(T)
    q = partial_rope(q.reshape(B, T, N_Q_HEADS, HEAD_DIM), pos)
    k = partial_rope(k.reshape(B, T, N_KV_HEADS, HEAD_DIM), pos)
    v = v.reshape(B, T, N_KV_HEADS, HEAD_DIM)
    if k_buf is None:
        att_out, k_new, v_new = swa_prompt(q, k, v, lw['attn_sinks'])
    else:
        att_out, k_new, v_new = swa_sample(q, k, v, k_buf, v_buf, pos0, lw['attn_sinks'])
    out = jnp.concatenate([rwkv_out, att_out], axis=-1) @ lw['w_out']
    return out, (S_new, u[:, -1], k_new, v_new)


def decoder_layer(x, c, S0, u_prev, k_buf, v_buf, pos0, lw):
    mods = (jax.nn.silu(c) @ lw['w_ada'] + lw['b_ada']).reshape(c.shape[0], N_ADA, 1, D_MODEL)
    sh1, sc1, gt1, sh2, sc2, gt2, sh3, sc3, gt3 = [mods[:, i] for i in range(N_ADA)]
    x = x + 0.5 * gt1 * swiglu(modulate(x, lw['norm_ffn1'], sh1, sc1), lw['ffn1_w_gu'], lw['ffn1_w_down'])
    m, st = parallel_heads(modulate(x, lw['norm_mix'], sh2, sc2), u_prev, S0, k_buf, v_buf, pos0, lw)
    x = x + gt2 * m
    x = x + 0.5 * gt3 * swiglu(modulate(x, lw['norm_ffn2'], sh3, sc3), lw['ffn2_w_gu'], lw['ffn2_w_down'])
    return x, st


def setup_inputs(seed: int = 0) -> dict:
    key = jax.random.key(seed)
    ks = iter(jax.random.split(key, 40))
    nrm = lambda shape, s: jax.random.normal(next(ks), shape, jnp.float32) * s
    uni = lambda shape, lo, hi: jax.random.uniform(next(ks), shape, jnp.float32, lo, hi)
    L = DEPTH
    return {
        'x_prompt': nrm((BATCH, SEQ, D_MODEL), 1.0),
        'x_sample': nrm((DEC_BATCH, DEC_SEQ, D_MODEL), 1.0),
        'c_prompt': nrm((BATCH, D_MODEL), 1.0),
        'c_sample': nrm((DEC_BATCH, D_MODEL), 1.0),
        'state_wkv': nrm((L, DEC_BATCH, N_RWKV_HEADS, HEAD_DIM, HEAD_DIM), 0.3),
        'state_shift': nrm((L, DEC_BATCH, RWKV_COLS), 1.0),
        'cache_k': nrm((L, DEC_BATCH, WINDOW, N_KV_HEADS, HEAD_DIM), 1.0),
        'cache_v': nrm((L, DEC_BATCH, WINDOW, N_KV_HEADS, HEAD_DIM), 1.0),
        'w_ada': nrm((L, D_MODEL, N_ADA * D_MODEL), 0.5 * D_MODEL ** -0.5),
        'b_ada': nrm((L, N_ADA * D_MODEL), 0.02),
        'norm_ffn1': 1.0 + nrm((L, D_MODEL), 0.02),
        'ffn1_w_gu': nrm((L, D_MODEL, 2 * D_FF), D_MODEL ** -0.5),
        'ffn1_w_down': nrm((L, D_FF, D_MODEL), D_FF ** -0.5),
        'norm_mix': 1.0 + nrm((L, D_MODEL), 0.02),
        'w_in': nrm((L, D_MODEL, IN_COLS), D_MODEL ** -0.5),
        'mu_shift': uni((L, RWKV_COLS), 0.0, 1.0),
        'rwkv_w0': uni((L, RWKV_WIDTH), -5.0, 0.0),
        'rwkv_w2': nrm((L, DECAY_LORA, RWKV_WIDTH), 0.5 * DECAY_LORA ** -0.5),
        'rwkv_a0': nrm((L, RWKV_WIDTH), 0.5),
        'rwkv_a2': nrm((L, AAA_LORA, RWKV_WIDTH), 0.5 * AAA_LORA ** -0.5),
        'rwkv_g2': nrm((L, GATE_LORA, RWKV_WIDTH), GATE_LORA ** -0.5),
        'rwkv_k_k': 0.85 + nrm((L, RWKV_WIDTH), 0.02),
        'rwkv_k_a': 1.0 + nrm((L, RWKV_WIDTH), 0.02),
        'rwkv_r_k': nrm((L, N_RWKV_HEADS, HEAD_DIM), 0.1),
        'ln_x_w': 1.0 + nrm((L, RWKV_WIDTH), 0.02),
        'ln_x_b': nrm((L, RWKV_WIDTH), 0.02),
        'attn_sinks': nrm((L, N_Q_HEADS), 1.0),
        'w_out': nrm((L, MIX_WIDTH, D_MODEL), MIX_WIDTH ** -0.5),
        'norm_ffn2': 1.0 + nrm((L, D_MODEL), 0.02),
        'ffn2_w_gu': nrm((L, D_MODEL, 2 * D_FF), D_MODEL ** -0.5),
        'ffn2_w_down': nrm((L, D_FF, D_MODEL), D_FF ** -0.5),
        'norm_final': 1.0 + nrm((D_MODEL,), 0.02),
    }


def reference(x_prompt, x_sample, c_prompt, c_sample, state_wkv, state_shift, cache_k, cache_v,
              w_ada, b_ada, norm_ffn1, ffn1_w_gu, ffn1_w_down, norm_mix, w_in, mu_shift,
              rwkv_w0, rwkv_w2, rwkv_a0, rwkv_a2, rwkv_g2, rwkv_k_k, rwkv_k_a, rwkv_r_k,
              ln_x_w, ln_x_b, attn_sinks, w_out, norm_ffn2, ffn2_w_gu, ffn2_w_down, norm_final):
    xp, xs = x_prompt, x_sample
    bp = x_prompt.shape[0]
    p_states, s_states = [], []
    for l in range(DEPTH):
        lw = {'w_ada': w_ada[l], 'b_ada': b_ada[l], 'norm_ffn1': norm_ffn1[l],
              'ffn1_w_gu': ffn1_w_gu[l], 'ffn1_w_down': ffn1_w_down[l], 'norm_mix': norm_mix[l],
              'w_in': w_in[l], 'mu_shift': mu_shift[l], 'rwkv_w0': rwkv_w0[l], 'rwkv_w2': rwkv_w2[l],
              'rwkv_a0': rwkv_a0[l], 'rwkv_a2': rwkv_a2[l], 'rwkv_g2': rwkv_g2[l],
              'rwkv_k_k': rwkv_k_k[l], 'rwkv_k_a': rwkv_k_a[l], 'rwkv_r_k': rwkv_r_k[l],
              'ln_x_w': ln_x_w[l], 'ln_x_b': ln_x_b[l], 'attn_sinks': attn_sinks[l],
              'w_out': w_out[l], 'norm_ffn2': norm_ffn2[l], 'ffn2_w_gu': ffn2_w_gu[l],
              'ffn2_w_down': ffn2_w_down[l]}
        S0_p = jnp.zeros((bp, N_RWKV_HEADS, HEAD_DIM, HEAD_DIM), jnp.float32)
        u0_p = jnp.zeros((bp, RWKV_COLS), xp.dtype)
        xp, st_p = decoder_layer(xp, c_prompt, S0_p, u0_p, None, None, 0, lw)
        xs, st_s = decoder_layer(xs, c_sample, state_wkv[l], state_shift[l], cache_k[l], cache_v[l], PAST_LEN, lw)
        p_states.append(st_p)
        s_states.append(st_s)
    y_prompt = rms_norm(xp, norm_final)
    y_sample = rms_norm(xs, norm_final)
    prompt_wkv = jnp.stack([st[0] for st in p_states])
    prompt_shift = jnp.stack([st[1] for st in p_states])
    prompt_k = jnp.stack([st[2] for st in p_states])
    prompt_v = jnp.stack([st[3] for st in p_states])
    sample_wkv = jnp.stack([st[0] for st in s_states])
    sample_shift = jnp.stack([st[1] for st in s_states])
    sample_k = jnp.stack([st[2] for st in s_states])
    sample_v = jnp.stack([st[3] for st in s_states])
    return (y_prompt, y_sample, prompt_wkv, prompt_shift, prompt_k, prompt_v,
            sample_wkv, sample_shift, sample_k, sample_v)
```

```python
import functools

import jax
import jax.numpy as jnp
from jax import lax
from jax.experimental import pallas as pl
from jax.experimental.pallas import tpu as pltpu

F32 = jnp.float32
BF16 = jnp.bfloat16

D_MODEL = 1024
HEAD_DIM = 64
N_RWKV_HEADS = 8
RWKV_WIDTH = N_RWKV_HEADS * HEAD_DIM
N_Q_HEADS = 8
N_KV_HEADS = 2
GROUP = N_Q_HEADS // N_KV_HEADS
ATT_WIDTH = N_Q_HEADS * HEAD_DIM
KV_WIDTH = N_KV_HEADS * HEAD_DIM
DECAY_LORA = 64
AAA_LORA = 64
GATE_LORA = 128
RWKV_COLS = 3 * RWKV_WIDTH + DECAY_LORA + AAA_LORA + GATE_LORA
IN_COLS = RWKV_COLS + ATT_WIDTH + 2 * KV_WIDTH
WINDOW = 128
ROPE_THETA = 500000.0
ROPE_DIM = HEAD_DIM // 4
D_FF = 2816
N_ADA = 9
NORM_EPS = 1e-5
LNX_EPS = 64e-5
NEG_INF = -1e30
PAST_LEN = 16384

V7X_VMEM_LIMIT_BYTES = 56 * 1024 * 1024
FFN_CHUNK = 256
SCAN_LANES = 256
SCAN_HEADS = SCAN_LANES // HEAD_DIM


def _cparams(sem):
    return pltpu.CompilerParams(dimension_semantics=sem, vmem_limit_bytes=V7X_VMEM_LIMIT_BYTES)


def _const_spec(shape):
    nd = len(shape)
    return pl.BlockSpec(shape, lambda *_: (0,) * nd)


def _dot(a, b):
    return jnp.dot(a, b, preferred_element_type=F32)


def _dot_nt(a, b):
    return lax.dot_general(a, b, (((1,), (1,)), ((), ())), preferred_element_type=F32)


def _split_sum(x, ones):
    hi = x.astype(BF16)
    lo = (x - hi.astype(F32)).astype(BF16)
    return _dot(hi, ones) + _dot(lo, ones)


def _mod_norm(x3, nw, sc, sh):
    ms = jnp.mean(x3 * x3, axis=-1, keepdims=True)
    y = x3 * lax.rsqrt(ms + NORM_EPS) * nw
    return y * (1.0 + sc) + sh


def _ada_kernel(c_ref, w_ref, b_ref, o_ref):
    c = c_ref[...]
    s = (c * jax.nn.sigmoid(c)).astype(BF16)
    o_ref[...] = _dot(s, w_ref[...].astype(BF16)) + b_ref[...]


def _ada(c_all, w_ada, b_ada):
    n = c_all.shape[0]
    cols = N_ADA * D_MODEL
    return pl.pallas_call(
        _ada_kernel,
        grid=(N_ADA,),
        in_specs=[
            pl.BlockSpec((n, D_MODEL), lambda i: (0, 0)),
            pl.BlockSpec((D_MODEL, D_MODEL), lambda i: (0, i)),
            pl.BlockSpec((1, D_MODEL), lambda i: (0, i)),
        ],
        out_specs=pl.BlockSpec((n, D_MODEL), lambda i: (0, i)),
        out_shape=jax.ShapeDtypeStruct((n, cols), F32),
        compiler_params=_cparams(("arbitrary",)),
        name="ada",
    )(c_all, w_ada, b_ada.reshape(1, cols))


def _ffn_kernel(x_ref, sh_ref, sc_ref, gt_ref, nw_ref, wgu_ref, wd_ref, fn_ref, o_ref, h_scr, acc_scr,
                *, final_norm):
    x3 = x_ref[...]
    ab, tb, _ = x3.shape
    h3 = _mod_norm(x3, nw_ref[...], sc_ref[...], sh_ref[...])
    h_scr[...] = h3.reshape(ab * tb, D_MODEL).astype(BF16)
    for c in range(D_FF // FFN_CHUNK):
        lo = c * FFN_CHUNK
        h = h_scr[...]
        g = _dot(h, wgu_ref[:, lo:lo + FFN_CHUNK])
        u = _dot(h, wgu_ref[:, D_FF + lo:D_FF + lo + FFN_CHUNK])
        act = (g * jax.nn.sigmoid(g) * u).astype(BF16)
        part = _dot(act, wd_ref[lo:lo + FFN_CHUNK, :])
        if c == 0:
            acc_scr[...] = part
        else:
            acc_scr[...] += part
    y3 = x3 + (0.5 * gt_ref[...]) * acc_scr[...].reshape(ab, tb, D_MODEL)
    if final_norm:
        ms = jnp.mean(y3 * y3, axis=-1, keepdims=True)
        y3 = y3 * lax.rsqrt(ms + NORM_EPS) * fn_ref[...]
    o_ref[...] = y3


def _ffn(x, mods, nw, wgu, wd, fnw, *, mod_base, ab, tb, final_norm):
    a_n, t_n, _ = x.shape
    grid = (a_n // ab, t_n // tb)
    mod_spec = lambda k: pl.BlockSpec((ab, 1, D_MODEL), lambda a, t, k=k: (a, 0, mod_base + k))
    return pl.pallas_call(
        functools.partial(_ffn_kernel, final_norm=final_norm),
        grid=grid,
        in_specs=[
            pl.BlockSpec((ab, tb, D_MODEL), lambda a, t: (a, t, 0)),
            mod_spec(0), mod_spec(1), mod_spec(2),
            _const_spec((1, 1, D_MODEL)),
            pl.BlockSpec((D_MODEL, 2 * D_FF), lambda a, t: (0, 0), pipeline_mode=pl.Buffered(1)),
            pl.BlockSpec((D_FF, D_MODEL), lambda a, t: (0, 0), pipeline_mode=pl.Buffered(1)),
            _const_spec((1, 1, D_MODEL)),
        ],
        out_specs=pl.BlockSpec((ab, tb, D_MODEL), lambda a, t: (a, t, 0)),
        out_shape=jax.ShapeDtypeStruct(x.shape, F32),
        scratch_shapes=[pltpu.VMEM((ab * tb, D_MODEL), BF16), pltpu.VMEM((ab * tb, D_MODEL), F32)],
        compiler_params=_cparams(("arbitrary", "arbitrary")),
        name="ffn_final" if final_norm else "ffn",
    )(x, mods, mods, mods, nw, wgu, wd, fnw)


def _mix_in_kernel(x_ref, sh_ref, sc_ref, nw_ref, win_ref, uprev_ref, mu_ref, w0_ref, w2_ref, a0_ref, a2_ref,
                   g2_ref, kk_ref, ka_ref, rk_ref, ones_ref, cq_ref, sqa_ref, sqb_ref, ck_ref, ska_ref, skb_ref,
                   r_out, d_out, k_out, v_out, a_out, b_out, g_out, bonus_out, q_out, ka_out, va_out, ulast_out,
                   carry_scr):
    t_id = pl.program_id(1)
    x3 = x_ref[...]
    ab, tb, _ = x3.shape
    tm = ab * tb
    h = _mod_norm(x3, nw_ref[...], sc_ref[...], sh_ref[...]).reshape(tm, D_MODEL).astype(BF16)
    proj = _dot(h, win_ref[...])

    u3 = proj[:, :RWKV_COLS].reshape(ab, tb, RWKV_COLS)

    @pl.when(t_id == 0)
    def _():
        carry_scr[...] = uprev_ref[...]

    prev = carry_scr[...]
    t_idx = lax.broadcasted_iota(jnp.int32, (ab, tb, RWKV_COLS), 1)
    u_shift = jnp.where(t_idx == 0, prev, pltpu.roll(u3, 1, 1))
    last = u3[:, tb - 1:tb, :]
    carry_scr[...] = last
    ulast_out[...] = last

    z = (u3 + (u_shift - u3) * mu_ref[...]).reshape(tm, RWKV_COLS)
    w = RWKV_WIDTH
    r = z[:, :w]
    k = z[:, w:2 * w]
    v = z[:, 2 * w:3 * w]
    zw = z[:, 3 * w:3 * w + DECAY_LORA]
    za = z[:, 3 * w + DECAY_LORA:3 * w + DECAY_LORA + AAA_LORA]
    zg = z[:, 3 * w + DECAY_LORA + AAA_LORA:]

    y = -(w0_ref[...] + _dot(jnp.tanh(zw).astype(BF16), w2_ref[...]))
    softplus = jnp.maximum(y, 0.0) + jnp.log1p(jnp.exp(-jnp.abs(y)))
    w_log = -softplus - 0.5
    decay = jnp.exp(-jnp.exp(w_log))
    a = jax.nn.sigmoid(a0_ref[...] + _dot(za.astype(BF16), a2_ref[...]))
    g = _dot(jax.nn.sigmoid(zg).astype(BF16), g2_ref[...])

    ones = ones_ref[...]
    kk = k * kk_ref[...]
    kk = kk * lax.rsqrt(jnp.maximum(_split_sum(kk * kk, ones), 1e-24))
    k_mod = k * (1.0 + (a - 1.0) * ka_ref[...])
    bonus = _split_sum(r * k_mod * rk_ref[...], ones) * v

    shp = (ab, tb, w)
    r_out[...] = r.reshape(shp)
    d_out[...] = decay.reshape(shp)
    k_out[...] = k_mod.reshape(shp)
    v_out[...] = v.reshape(shp)
    a_out[...] = (-kk).reshape(shp)
    b_out[...] = (kk * a).reshape(shp)
    g_out[...] = g.reshape(shp)
    bonus_out[...] = bonus.reshape(shp)

    q3 = proj[:, RWKV_COLS:RWKV_COLS + ATT_WIDTH].reshape(ab, tb, ATT_WIDTH)
    q_out[...] = (q3 * cq_ref[...] + pltpu.roll(q3, ROPE_DIM // 2, 2) * sqa_ref[...]
                  + pltpu.roll(q3, ATT_WIDTH - ROPE_DIM // 2, 2) * sqb_ref[...])
    k3 = proj[:, RWKV_COLS + ATT_WIDTH:RWKV_COLS + ATT_WIDTH + KV_WIDTH].reshape(ab, tb, KV_WIDTH)
    ka_out[...] = (k3 * ck_ref[...] + pltpu.roll(k3, ROPE_DIM // 2, 2) * ska_ref[...]
                   + pltpu.roll(k3, KV_WIDTH - ROPE_DIM // 2, 2) * skb_ref[...])
    va_out[...] = proj[:, RWKV_COLS + ATT_WIDTH + KV_WIDTH:].reshape(ab, tb, KV_WIDTH)


def _mix_in(x, mods, u_prev0, p, rope, *, ab, tb):
    a_n, t_n, _ = x.shape
    grid = (a_n // ab, t_n // tb)
    tok = lambda c: pl.BlockSpec((ab, tb, c), lambda a, t: (a, t, 0))
    seq = lambda c: pl.BlockSpec((ab, 1, c), lambda a, t: (a, 0, 0))
    mod_spec = lambda k: pl.BlockSpec((ab, 1, D_MODEL), lambda a, t, k=k: (a, 0, k))
    rope_spec = lambda c: pl.BlockSpec((1, tb, c), lambda a, t: (0, t, 0))
    w = RWKV_WIDTH
    tok_shape = lambda c: jax.ShapeDtypeStruct((a_n, t_n, c), F32)
    return pl.pallas_call(
        _mix_in_kernel,
        grid=grid,
        in_specs=[
            tok(D_MODEL), mod_spec(3), mod_spec(4),
            _const_spec((1, 1, D_MODEL)),
            pl.BlockSpec((D_MODEL, IN_COLS), lambda a, t: (0, 0), pipeline_mode=pl.Buffered(1)),
            seq(RWKV_COLS),
            _const_spec((1, 1, RWKV_COLS)),
            _const_spec((1, w)), _const_spec((DECAY_LORA, w)),
            _const_spec((1, w)), _const_spec((AAA_LORA, w)),
            _const_spec((GATE_LORA, w)),
            _const_spec((1, w)), _const_spec((1, w)), _const_spec((1, w)),
            _const_spec((w, w)),
            rope_spec(ATT_WIDTH), rope_spec(ATT_WIDTH), rope_spec(ATT_WIDTH),
            rope_spec(KV_WIDTH), rope_spec(KV_WIDTH), rope_spec(KV_WIDTH),
        ],
        out_specs=[tok(w)] * 8 + [tok(ATT_WIDTH), tok(KV_WIDTH), tok(KV_WIDTH), seq(RWKV_COLS)],
        out_shape=[tok_shape(w)] * 8 + [tok_shape(ATT_WIDTH), tok_shape(KV_WIDTH), tok_shape(KV_WIDTH),
                                        jax.ShapeDtypeStruct((a_n, 1, RWKV_COLS), F32)],
        scratch_shapes=[pltpu.VMEM((ab, 1, RWKV_COLS), F32)],
        compiler_params=_cparams(("arbitrary", "arbitrary")),
        name="mix_in",
    )(x, mods, mods, p["norm_mix"], p["w_in"], u_prev0, p["mu_shift"], p["rwkv_w0"], p["rwkv_w2"],
      p["rwkv_a0"], p["rwkv_a2"], p["rwkv_g2"], p["rwkv_k_k"], p["rwkv_k_a"], p["rwkv_r_k"], p["head_ones"],
      *rope)


def _scan_kernel(r_ref, d_ref, k_ref, v_ref, a_ref, b_ref, s0_ref, ones_ref, o_ref, sfin_ref, s_scr, *, ab, tb):
    t_id = pl.program_id(1)

    @pl.when(t_id == 0)
    def _():
        s_scr[...] = s0_ref[...]

    ones = ones_ref[...]
    row_i = lax.broadcasted_iota(jnp.int32, (HEAD_DIM, SCAN_LANES), 0)
    lane_i = lax.broadcasted_iota(jnp.int32, (HEAD_DIM, SCAN_LANES), 1)
    diag = row_i == (lane_i & (HEAD_DIM - 1))
    n_groups = RWKV_WIDTH // SCAN_LANES

    def tile_body(tt, carry):
        base = pl.multiple_of(tt * 8, 8)
        for b in range(ab):
            for hg in range(n_groups):
                ls = slice(SCAN_LANES * hg, SCAN_LANES * (hg + 1))
                rr = r_ref[b, pl.ds(base, 8), ls]
                dd = d_ref[b, pl.ds(base, 8), ls]
                kk = k_ref[b, pl.ds(base, 8), ls]
                vv = v_ref[b, pl.ds(base, 8), ls]
                aa = a_ref[b, pl.ds(base, 8), ls]
                bb = b_ref[b, pl.ds(base, 8), ls]
                s = s_scr[b, hg]
                rows = []
                for i in range(8):
                    v_col = _split_sum(jnp.where(diag, vv[i:i + 1, :], 0.0), ones)
                    sa = _split_sum(s * aa[i:i + 1, :], ones)
                    s = s * dd[i:i + 1, :] + sa * bb[i:i + 1, :] + v_col * kk[i:i + 1, :]
                    o_b = _split_sum(s * rr[i:i + 1, :], ones)
                    rows.append(jnp.sum(jnp.where(diag, o_b, 0.0), axis=0, keepdims=True))
                s_scr[b, hg] = s
                o_ref[b, pl.ds(base, 8), ls] = jnp.concatenate(rows, axis=0)
        return carry

    lax.fori_loop(0, tb // 8, tile_body, 0)

    @pl.when(t_id == pl.num_programs(1) - 1)
    def _():
        sfin_ref[...] = s_scr[...]


def _scan(r, d, k, v, a, b, s0, ones, *, ab, tb):
    a_n, t_n, w = r.shape
    n_groups = w // SCAN_LANES
    grid = (a_n // ab, t_n // tb)
    tok = pl.BlockSpec((ab, tb, w), lambda i, t: (i, t, 0))
    st = pl.BlockSpec((ab, n_groups, HEAD_DIM, SCAN_LANES), lambda i, t: (i, 0, 0, 0))
    return pl.pallas_call(
        functools.partial(_scan_kernel, ab=ab, tb=tb),
        grid=grid,
        in_specs=[tok] * 6 + [st, _const_spec((SCAN_LANES, SCAN_LANES))],
        out_specs=[tok, st],
        out_shape=[jax.ShapeDtypeStruct(r.shape, F32), jax.ShapeDtypeStruct(s0.shape, F32)],
        scratch_shapes=[pltpu.VMEM((ab, n_groups, HEAD_DIM, SCAN_LANES), F32)],
        compiler_params=_cparams(("arbitrary", "arbitrary")),
        name="wkv_scan",
    )(r, d, k, v, a, b, s0, ones)


def _attn_kernel(sink_ref, q_ref, kp_ref, vp_ref, kc_ref, vc_ref, o_ref, *, sb, tq, block_prev):
    rows = GROUP * tq
    assert tq & (tq - 1) == 0
    t_row = lax.broadcasted_iota(jnp.int32, (rows, WINDOW), 0) & (tq - 1)
    c_prev = lax.broadcasted_iota(jnp.int32, (rows, WINDOW), 1)
    if block_prev:
        t_row = t_row + jnp.where(pl.program_id(1) >= 1, 0, WINDOW)
    prev_mask = c_prev > t_row
    t_row_c = lax.broadcasted_iota(jnp.int32, (rows, tq), 0) & (tq - 1)
    c_cur = lax.broadcasted_iota(jnp.int32, (rows, tq), 1)
    cur_mask = c_cur <= t_row_c
    grp = lax.broadcasted_iota(jnp.int32, (rows, 1), 0) >> (tq.bit_length() - 1)
    scale = HEAD_DIM ** -0.5
    for s in range(sb):
        q = q_ref[s]
        outs = []
        for h2 in range(N_KV_HEADS):
            hs = slice(HEAD_DIM * h2, HEAD_DIM * (h2 + 1))
            kp = kp_ref[s, :, hs].astype(BF16)
            vp = vp_ref[s, :, hs].astype(BF16)
            kc = kc_ref[s, :, hs].astype(BF16)
            vc = vc_ref[s, :, hs].astype(BF16)
            qh = jnp.concatenate(
                [q[:, HEAD_DIM * (h2 * GROUP + g):HEAD_DIM * (h2 * GROUP + g + 1)] for g in range(GROUP)], axis=0)
            qh = qh.astype(BF16)
            sink = jnp.zeros((rows, 1), F32)
            for g in range(GROUP):
                sink = jnp.where(grp == g, sink_ref[h2 * GROUP + g], sink)
            s_p = jnp.where(prev_mask, _dot_nt(qh, kp) * scale, NEG_INF)
            s_c = jnp.where(cur_mask, _dot_nt(qh, kc) * scale, NEG_INF)
            m = jnp.maximum(jnp.maximum(jnp.max(s_p, axis=-1, keepdims=True),
                                        jnp.max(s_c, axis=-1, keepdims=True)), sink)
            p_p = jnp.exp(s_p - m)
            p_c = jnp.exp(s_c - m)
            den = (jnp.sum(p_p, axis=-1, keepdims=True) + jnp.sum(p_c, axis=-1, keepdims=True)
                   + jnp.exp(sink - m))
            inv = 1.0 / den
            o = _dot((p_p * inv).astype(BF16), vp) + _dot((p_c * inv).astype(BF16), vc)
            outs.extend(o[g * tq:(g + 1) * tq, :] for g in range(GROUP))
        o_ref[s] = jnp.concatenate(outs, axis=-1)


def _attn(sinks, q, k_prev, v_prev, k_cur, v_cur, *, sb, tq, block_prev):
    a_n, t_n, _ = q.shape
    grid = (a_n // sb, t_n // tq)
    cur = lambda c: pl.BlockSpec((sb, tq, c), lambda a, t: (a, t, 0))
    if block_prev:
        prev = pl.BlockSpec((sb, WINDOW, KV_WIDTH), lambda a, t: (a, jnp.maximum(t - 1, 0), 0))
    else:
        prev = pl.BlockSpec((sb, WINDOW, KV_WIDTH), lambda a, t: (a, 0, 0))
    return pl.pallas_call(
        functools.partial(_attn_kernel, sb=sb, tq=tq, block_prev=block_prev),
        grid=grid,
        in_specs=[pl.BlockSpec(memory_space=pltpu.SMEM), cur(ATT_WIDTH), prev, prev, cur(KV_WIDTH), cur(KV_WIDTH)],
        out_specs=cur(ATT_WIDTH),
        out_shape=jax.ShapeDtypeStruct(q.shape, F32),
        compiler_params=_cparams(("arbitrary", "arbitrary")),
        name="swa_attn",
    )(sinks, q, k_prev, v_prev, k_cur, v_cur)


def _mix_out_kernel(x_ref, gt_ref, o_ref, bonus_ref, g_ref, att_ref, lnw_ref, lnb_ref, ones_ref, wo_ref, y_ref):
    x3 = x_ref[...]
    ab, tb, _ = x3.shape
    tm = ab * tb
    ones = ones_ref[...]
    o = o_ref[...].reshape(tm, RWKV_WIDTH)
    mu = _split_sum(o, ones) * (1.0 / HEAD_DIM)
    oc = o - mu
    var = _split_sum(oc * oc, ones) * (1.0 / HEAD_DIM)
    on = oc * lax.rsqrt(var + LNX_EPS) * lnw_ref[...] + lnb_ref[...]
    rw = (on + bonus_ref[...].reshape(tm, RWKV_WIDTH)) * g_ref[...].reshape(tm, RWKV_WIDTH)
    att = att_ref[...].reshape(tm, ATT_WIDTH)
    m = _dot(rw.astype(BF16), wo_ref[:RWKV_WIDTH, :]) + _dot(att.astype(BF16), wo_ref[RWKV_WIDTH:, :])
    y_ref[...] = x3 + gt_ref[...] * m.reshape(ab, tb, D_MODEL)


def _mix_out(x, mods, o, bonus, g, att, p, *, ab, tb):
    a_n, t_n, _ = x.shape
    grid = (a_n // ab, t_n // tb)
    tok = lambda c: pl.BlockSpec((ab, tb, c), lambda a, t: (a, t, 0))
    w = RWKV_WIDTH
    return pl.pallas_call(
        _mix_out_kernel,
        grid=grid,
        in_specs=[
            tok(D_MODEL),
            pl.BlockSpec((ab, 1, D_MODEL), lambda a, t: (a, 0, 5)),
            tok(w), tok(w), tok(w), tok(ATT_WIDTH),
            _const_spec((1, w)), _const_spec((1, w)), _const_spec((w, w)),
            _const_spec((w + ATT_WIDTH, D_MODEL)),
        ],
        out_specs=tok(D_MODEL),
        out_shape=jax.ShapeDtypeStruct(x.shape, F32),
        compiler_params=_cparams(("arbitrary", "arbitrary")),
        name="mix_out",
    )(x, mods, o, bonus, g, att, p["ln_x_w"], p["ln_x_b"], p["head_ones"], p["w_out"])


def _rope_tables(pos, width):
    half = ROPE_DIM // 2
    inv_freq = ROPE_THETA ** (-jnp.arange(0, ROPE_DIM, 2, dtype=F32) / ROPE_DIM)
    ang = pos.astype(F32)[:, None] * inv_freq[None, :]
    cos, sin = jnp.cos(ang), jnp.sin(ang)
    t = pos.shape[0]
    pad = jnp.zeros((t, HEAD_DIM - ROPE_DIM), F32)
    zero = jnp.zeros((t, half), F32)
    c_head = jnp.concatenate([cos, cos, pad + 1.0], axis=-1)
    sa_head = jnp.concatenate([zero, sin, pad], axis=-1)
    sb_head = jnp.concatenate([-sin, zero, pad], axis=-1)
    reps = width // HEAD_DIM
    tile = lambda x: jnp.tile(x, (1, reps))[None]
    return tile(c_head), tile(sa_head), tile(sb_head)


def _state_to_lanes(s):
    a_n = s.shape[0]
    s = s.reshape(a_n, N_RWKV_HEADS // SCAN_HEADS, SCAN_HEADS, HEAD_DIM, HEAD_DIM)
    return s.transpose(0, 1, 3, 2, 4).reshape(a_n, N_RWKV_HEADS // SCAN_HEADS, HEAD_DIM, SCAN_LANES)


def _state_from_lanes(s):
    a_n = s.shape[0]
    s = s.reshape(a_n, N_RWKV_HEADS // SCAN_HEADS, HEAD_DIM, SCAN_HEADS, HEAD_DIM)
    return s.transpose(0, 1, 3, 2, 4).reshape(a_n, N_RWKV_HEADS, HEAD_DIM, HEAD_DIM)


def _layer(x, mods, s0, u_prev0, k_buf, v_buf, pos0, p, *, ab, tb, scan_ab, scan_tb, attn_sb, final_norm_w):
    a_n, t_n, _ = x.shape
    x = _ffn(x, mods, p["norm_ffn1"], p["ffn1_w_gu"], p["ffn1_w_down"], final_norm_w,
             mod_base=0, ab=ab, tb=tb, final_norm=False)
    pos = pos0 + jnp.arange(t_n)
    rope = _rope_tables(pos, ATT_WIDTH) + _rope_tables(pos, KV_WIDTH)
    mix_ab, mix_tb = (ab // 2, tb) if ab > 1 else (ab, tb // 2)
    (r, d, k, v, a, b, g, bonus, q, k_att, v_att, u_last) = _mix_in(x, mods, u_prev0, p, rope,
                                                                   ab=mix_ab, tb=mix_tb)
    o, s_fin = _scan(r, d, k, v, a, b, _state_to_lanes(s0), p["scan_ones"], ab=scan_ab, tb=scan_tb)
    if k_buf is None:
        att = _attn(p["attn_sinks"], q, k_att, v_att, k_att, v_att, sb=1, tq=WINDOW, block_prev=True)
        k_new, v_new = k_att[:, -WINDOW:], v_att[:, -WINDOW:]
    else:
        kb = k_buf.reshape(a_n, WINDOW, KV_WIDTH)
        vb = v_buf.reshape(a_n, WINDOW, KV_WIDTH)
        att = _attn(p["attn_sinks"], q, kb, vb, k_att, v_att, sb=attn_sb, tq=t_n, block_prev=False)
        k_new = jnp.concatenate([kb, k_att], axis=1)[:, -WINDOW:]
        v_new = jnp.concatenate([vb, v_att], axis=1)[:, -WINDOW:]
    x = _mix_out(x, mods, o, bonus, g, att, p, ab=ab, tb=tb)
    y = _ffn(x, mods, p["norm_ffn2"], p["ffn2_w_gu"], p["ffn2_w_down"], final_norm_w,
             mod_base=6, ab=ab, tb=tb, final_norm=True)
    kv_shape = (a_n, WINDOW, N_KV_HEADS, HEAD_DIM)
    return y, (_state_from_lanes(s_fin), u_last[:, 0, :], k_new.reshape(kv_shape), v_new.reshape(kv_shape))


def kernel(x_prompt, x_sample, c_prompt, c_sample, state_wkv, state_shift, cache_k, cache_v, w_ada, b_ada,
           norm_ffn1, ffn1_w_gu, ffn1_w_down, norm_mix, w_in, mu_shift, rwkv_w0, rwkv_w2, rwkv_a0, rwkv_a2,
           rwkv_g2, rwkv_k_k, rwkv_k_a, rwkv_r_k, ln_x_w, ln_x_b, attn_sinks, w_out, norm_ffn2, ffn2_w_gu,
           ffn2_w_down, norm_final):
    depth = w_ada.shape[0]
    assert depth == 1, "final norm is fused into the last layer's second FFN"
    bp = x_prompt.shape[0]
    bs = x_sample.shape[0]
    w = RWKV_WIDTH
    lane_head = jnp.arange(w) // HEAD_DIM
    head_ones = (lane_head[:, None] == lane_head[None, :]).astype(BF16)
    fnw = norm_final.reshape(1, 1, D_MODEL)

    n_c = bp + bs
    n_pad = -n_c % 8
    c_all = jnp.concatenate([c_prompt, c_sample, jnp.zeros((n_pad, D_MODEL), F32)], axis=0)

    l = 0
    p = {
        "norm_ffn1": norm_ffn1[l].reshape(1, 1, D_MODEL),
        "ffn1_w_gu": ffn1_w_gu[l].astype(BF16), "ffn1_w_down": ffn1_w_down[l].astype(BF16),
        "norm_mix": norm_mix[l].reshape(1, 1, D_MODEL),
        "w_in": w_in[l].astype(BF16),
        "mu_shift": mu_shift[l].reshape(1, 1, RWKV_COLS),
        "rwkv_w0": rwkv_w0[l].reshape(1, w), "rwkv_w2": rwkv_w2[l].astype(BF16),
        "rwkv_a0": rwkv_a0[l].reshape(1, w), "rwkv_a2": rwkv_a2[l].astype(BF16),
        "rwkv_g2": rwkv_g2[l].astype(BF16),
        "rwkv_k_k": rwkv_k_k[l].reshape(1, w), "rwkv_k_a": rwkv_k_a[l].reshape(1, w),
        "rwkv_r_k": rwkv_r_k[l].reshape(1, w),
        "ln_x_w": ln_x_w[l].reshape(1, w), "ln_x_b": ln_x_b[l].reshape(1, w),
        "attn_sinks": attn_sinks[l],
        "w_out": w_out[l].astype(BF16),
        "norm_ffn2": norm_ffn2[l].reshape(1, 1, D_MODEL),
        "ffn2_w_gu": ffn2_w_gu[l].astype(BF16), "ffn2_w_down": ffn2_w_down[l].astype(BF16),
        "head_ones": head_ones,
        "scan_ones": head_ones[:SCAN_LANES, :SCAN_LANES],
    }
    mods_all = _ada(c_all, w_ada[l], b_ada[l])
    mods_p = mods_all[:bp].reshape(bp, 1, N_ADA * D_MODEL)
    mods_s = mods_all[bp:n_c].reshape(bs, 1, N_ADA * D_MODEL)

    s0_p = jnp.zeros((bp, N_RWKV_HEADS, HEAD_DIM, HEAD_DIM), F32)
    u0_p = jnp.zeros((bp, 1, RWKV_COLS), F32)
    yp, st_p = _layer(x_prompt, mods_p, s0_p, u0_p, None, None, 0, p,
                      ab=1, tb=512, scan_ab=bp, scan_tb=128, attn_sb=1, final_norm_w=fnw)
    t_s = x_sample.shape[1]
    ys, st_s = _layer(x_sample, mods_s, state_wkv[l], state_shift[l].reshape(bs, 1, RWKV_COLS),
                      cache_k[l], cache_v[l], PAST_LEN, p,
                      ab=512 // t_s, tb=t_s, scan_ab=4, scan_tb=t_s, attn_sb=8, final_norm_w=fnw)
    return (yp, ys, st_p[0][None], st_p[1][None], st_p[2][None], st_p[3][None],
            st_s[0][None], st_s[1][None], st_s[2][None], st_s[3][None])
```

```python
import functools

import jax
import jax.numpy as jnp
from jax import lax
from jax.experimental import pallas as pl
from jax.experimental.pallas import tpu as pltpu

F32 = jnp.float32
BF16 = jnp.bfloat16

D_MODEL = 1024
HEAD_DIM = 64
N_RWKV_HEADS = 8
RWKV_WIDTH = N_RWKV_HEADS * HEAD_DIM
N_Q_HEADS = 8
N_KV_HEADS = 2
GROUP = N_Q_HEADS // N_KV_HEADS
ATT_WIDTH = N_Q_HEADS * HEAD_DIM
KV_WIDTH = N_KV_HEADS * HEAD_DIM
DECAY_LORA = 64
AAA_LORA = 64
GATE_LORA = 128
RWKV_COLS = 3 * RWKV_WIDTH + DECAY_LORA + AAA_LORA + GATE_LORA
IN_COLS = RWKV_COLS + ATT_WIDTH + 2 * KV_WIDTH
WINDOW = 128
ROPE_THETA = 500000.0
ROPE_DIM = HEAD_DIM // 4
D_FF = 2816
N_ADA = 9
NORM_EPS = 1e-5
LNX_EPS = 64e-5
NEG_INF = -1e30
PAST_LEN = 16384

V7X_VMEM_LIMIT_BYTES = 56 * 1024 * 1024
FFN_CHUNK = 256
SCAN_LANES = 256
SCAN_HEADS = SCAN_LANES // HEAD_DIM
SCAN_CHUNK = 64
PAIR_LANES = 2 * HEAD_DIM


def _cparams(sem):
    return pltpu.CompilerParams(dimension_semantics=sem, vmem_limit_bytes=V7X_VMEM_LIMIT_BYTES)


def _const_spec(shape):
    nd = len(shape)
    return pl.BlockSpec(shape, lambda *_: (0,) * nd)


def _dot(a, b):
    return jnp.dot(a, b, preferred_element_type=F32)


def _dot_nt(a, b):
    return lax.dot_general(a, b, (((1,), (1,)), ((), ())), preferred_element_type=F32)


def _split_sum(x, ones):
    hi = x.astype(BF16)
    lo = (x - hi.astype(F32)).astype(BF16)
    return _dot(hi, ones) + _dot(lo, ones)


def _mod_norm(x3, nw, sc, sh):
    ms = jnp.mean(x3 * x3, axis=-1, keepdims=True)
    y = x3 * lax.rsqrt(ms + NORM_EPS) * nw
    return y * (1.0 + sc) + sh


def _ada_kernel(c_ref, w_ref, b_ref, o_ref):
    c = c_ref[...]
    s = (c * jax.nn.sigmoid(c)).astype(BF16)
    o_ref[...] = _dot(s, w_ref[...].astype(BF16)) + b_ref[...]


def _ada(c_all, w_ada, b_ada):
    n = c_all.shape[0]
    cols = N_ADA * D_MODEL
    return pl.pallas_call(
        _ada_kernel,
        grid=(N_ADA,),
        in_specs=[
            pl.BlockSpec((n, D_MODEL), lambda i: (0, 0)),
            pl.BlockSpec((D_MODEL, D_MODEL), lambda i: (0, i)),
            pl.BlockSpec((1, D_MODEL), lambda i: (0, i)),
        ],
        out_specs=pl.BlockSpec((n, D_MODEL), lambda i: (0, i)),
        out_shape=jax.ShapeDtypeStruct((n, cols), F32),
        compiler_params=_cparams(("arbitrary",)),
        name="ada",
    )(c_all, w_ada, b_ada.reshape(1, cols))


def _ffn_kernel(x_ref, sh_ref, sc_ref, gt_ref, nw_ref, wgu_ref, wd_ref, fn_ref, o_ref, h_scr, acc_scr,
                *, final_norm):
    x3 = x_ref[...]
    ab, tb, _ = x3.shape
    h3 = _mod_norm(x3, nw_ref[...], sc_ref[...], sh_ref[...])
    h_scr[...] = h3.reshape(ab * tb, D_MODEL).astype(BF16)
    for c in range(D_FF // FFN_CHUNK):
        lo = c * FFN_CHUNK
        h = h_scr[...]
        g = _dot(h, wgu_ref[:, lo:lo + FFN_CHUNK])
        u = _dot(h, wgu_ref[:, D_FF + lo:D_FF + lo + FFN_CHUNK])
        act = (g * jax.nn.sigmoid(g) * u).astype(BF16)
        part = _dot(act, wd_ref[lo:lo + FFN_CHUNK, :])
        if c == 0:
            acc_scr[...] = part
        else:
            acc_scr[...] += part
    y3 = x3 + (0.5 * gt_ref[...]) * acc_scr[...].reshape(ab, tb, D_MODEL)
    if final_norm:
        ms = jnp.mean(y3 * y3, axis=-1, keepdims=True)
        y3 = y3 * lax.rsqrt(ms + NORM_EPS) * fn_ref[...]
    o_ref[...] = y3


def _ffn(x, mods, nw, wgu, wd, fnw, *, mod_base, ab, tb, final_norm):
    a_n, t_n, _ = x.shape
    grid = (a_n // ab, t_n // tb)
    mod_spec = lambda k: pl.BlockSpec((ab, 1, D_MODEL), lambda a, t, k=k: (a, 0, mod_base + k))
    return pl.pallas_call(
        functools.partial(_ffn_kernel, final_norm=final_norm),
        grid=grid,
        in_specs=[
            pl.BlockSpec((ab, tb, D_MODEL), lambda a, t: (a, t, 0)),
            mod_spec(0), mod_spec(1), mod_spec(2),
            _const_spec((1, 1, D_MODEL)),
            pl.BlockSpec((D_MODEL, 2 * D_FF), lambda a, t: (0, 0), pipeline_mode=pl.Buffered(1)),
            pl.BlockSpec((D_FF, D_MODEL), lambda a, t: (0, 0), pipeline_mode=pl.Buffered(1)),
            _const_spec((1, 1, D_MODEL)),
        ],
        out_specs=pl.BlockSpec((ab, tb, D_MODEL), lambda a, t: (a, t, 0)),
        out_shape=jax.ShapeDtypeStruct(x.shape, F32),
        scratch_shapes=[pltpu.VMEM((ab * tb, D_MODEL), BF16), pltpu.VMEM((ab * tb, D_MODEL), F32)],
        compiler_params=_cparams(("arbitrary", "arbitrary")),
        name="ffn_final" if final_norm else "ffn",
    )(x, mods, mods, mods, nw, wgu, wd, fnw)


def _mix_in_kernel(x_ref, sh_ref, sc_ref, nw_ref, win_ref, uprev_ref, mu_ref, w0_ref, w2_ref, a0_ref, a2_ref,
                   g2_ref, kk_ref, ka_ref, rk_ref, ones_ref, cq_ref, sqa_ref, sqb_ref, ck_ref, ska_ref, skb_ref,
                   r_out, d_out, k_out, v_out, a_out, b_out, g_out, bonus_out, q_out, ka_out, va_out, ulast_out,
                   carry_scr):
    t_id = pl.program_id(1)
    x3 = x_ref[...]
    ab, tb, _ = x3.shape
    tm = ab * tb
    h = _mod_norm(x3, nw_ref[...], sc_ref[...], sh_ref[...]).reshape(tm, D_MODEL).astype(BF16)
    proj = _dot(h, win_ref[...])

    u3 = proj[:, :RWKV_COLS].reshape(ab, tb, RWKV_COLS)

    @pl.when(t_id == 0)
    def _():
        carry_scr[...] = uprev_ref[...]

    prev = carry_scr[...]
    t_idx = lax.broadcasted_iota(jnp.int32, (ab, tb, RWKV_COLS), 1)
    u_shift = jnp.where(t_idx == 0, prev, pltpu.roll(u3, 1, 1))
    last = u3[:, tb - 1:tb, :]
    carry_scr[...] = last
    ulast_out[...] = last

    z = (u3 + (u_shift - u3) * mu_ref[...]).reshape(tm, RWKV_COLS)
    w = RWKV_WIDTH
    r = z[:, :w]
    k = z[:, w:2 * w]
    v = z[:, 2 * w:3 * w]
    zw = z[:, 3 * w:3 * w + DECAY_LORA]
    za = z[:, 3 * w + DECAY_LORA:3 * w + DECAY_LORA + AAA_LORA]
    zg = z[:, 3 * w + DECAY_LORA + AAA_LORA:]

    y = -(w0_ref[...] + _dot(jnp.tanh(zw).astype(BF16), w2_ref[...]))
    softplus = jnp.maximum(y, 0.0) + jnp.log1p(jnp.exp(-jnp.abs(y)))
    w_log = -softplus - 0.5
    log_decay = -jnp.exp(w_log)
    a = jax.nn.sigmoid(a0_ref[...] + _dot(za.astype(BF16), a2_ref[...]))
    g = _dot(jax.nn.sigmoid(zg).astype(BF16), g2_ref[...])

    ones = ones_ref[...]
    kk = k * kk_ref[...]
    kk = kk * lax.rsqrt(jnp.maximum(_split_sum(kk * kk, ones), 1e-24))
    k_mod = k * (1.0 + (a - 1.0) * ka_ref[...])
    bonus = _split_sum(r * k_mod * rk_ref[...], ones) * v

    shp = (ab, tb, w)
    r_out[...] = r.reshape(shp)
    d_out[...] = log_decay.reshape(shp)
    k_out[...] = k_mod.reshape(shp)
    v_out[...] = v.reshape(shp)
    a_out[...] = (-kk).reshape(shp)
    b_out[...] = (kk * a).reshape(shp)
    g_out[...] = g.reshape(shp)
    bonus_out[...] = bonus.reshape(shp)

    q3 = proj[:, RWKV_COLS:RWKV_COLS + ATT_WIDTH].reshape(ab, tb, ATT_WIDTH)
    q_out[...] = (q3 * cq_ref[...] + pltpu.roll(q3, ROPE_DIM // 2, 2) * sqa_ref[...]
                  + pltpu.roll(q3, ATT_WIDTH - ROPE_DIM // 2, 2) * sqb_ref[...])
    k3 = proj[:, RWKV_COLS + ATT_WIDTH:RWKV_COLS + ATT_WIDTH + KV_WIDTH].reshape(ab, tb, KV_WIDTH)
    ka_out[...] = (k3 * ck_ref[...] + pltpu.roll(k3, ROPE_DIM // 2, 2) * ska_ref[...]
                   + pltpu.roll(k3, KV_WIDTH - ROPE_DIM // 2, 2) * skb_ref[...])
    va_out[...] = proj[:, RWKV_COLS + ATT_WIDTH + KV_WIDTH:].reshape(ab, tb, KV_WIDTH)


def _mix_in(x, mods, u_prev0, p, rope, *, ab, tb):
    a_n, t_n, _ = x.shape
    grid = (a_n // ab, t_n // tb)
    tok = lambda c: pl.BlockSpec((ab, tb, c), lambda a, t: (a, t, 0))
    seq = lambda c: pl.BlockSpec((ab, 1, c), lambda a, t: (a, 0, 0))
    mod_spec = lambda k: pl.BlockSpec((ab, 1, D_MODEL), lambda a, t, k=k: (a, 0, k))
    rope_spec = lambda c: pl.BlockSpec((1, tb, c), lambda a, t: (0, t, 0))
    w = RWKV_WIDTH
    tok_shape = lambda c: jax.ShapeDtypeStruct((a_n, t_n, c), F32)
    return pl.pallas_call(
        _mix_in_kernel,
        grid=grid,
        in_specs=[
            tok(D_MODEL), mod_spec(3), mod_spec(4),
            _const_spec((1, 1, D_MODEL)),
            pl.BlockSpec((D_MODEL, IN_COLS), lambda a, t: (0, 0), pipeline_mode=pl.Buffered(1)),
            seq(RWKV_COLS),
            _const_spec((1, 1, RWKV_COLS)),
            _const_spec((1, w)), _const_spec((DECAY_LORA, w)),
            _const_spec((1, w)), _const_spec((AAA_LORA, w)),
            _const_spec((GATE_LORA, w)),
            _const_spec((1, w)), _const_spec((1, w)), _const_spec((1, w)),
            _const_spec((w, w)),
            rope_spec(ATT_WIDTH), rope_spec(ATT_WIDTH), rope_spec(ATT_WIDTH),
            rope_spec(KV_WIDTH), rope_spec(KV_WIDTH), rope_spec(KV_WIDTH),
        ],
        out_specs=[tok(w)] * 8 + [tok(ATT_WIDTH), tok(KV_WIDTH), tok(KV_WIDTH), seq(RWKV_COLS)],
        out_shape=[tok_shape(w)] * 8 + [tok_shape(ATT_WIDTH), tok_shape(KV_WIDTH), tok_shape(KV_WIDTH),
                                        jax.ShapeDtypeStruct((a_n, 1, RWKV_COLS), F32)],
        scratch_shapes=[pltpu.VMEM((ab, 1, RWKV_COLS), F32)],
        compiler_params=_cparams(("arbitrary", "arbitrary")),
        name="mix_in",
    )(x, mods, mods, p["norm_mix"], p["w_in"], u_prev0, p["mu_shift"], p["rwkv_w0"], p["rwkv_w2"],
      p["rwkv_a0"], p["rwkv_a2"], p["rwkv_g2"], p["rwkv_k_k"], p["rwkv_k_a"], p["rwkv_r_k"], p["head_ones"],
      *rope)


def _scan_kernel(r_ref, d_ref, k_ref, v_ref, a_ref, b_ref, s0_ref, ones_ref, o_ref, sfin_ref, s_scr, *, ab, tb):
    t_id = pl.program_id(1)

    @pl.when(t_id == 0)
    def _():
        s_scr[...] = s0_ref[...]

    ones = ones_ref[...]
    row_i = lax.broadcasted_iota(jnp.int32, (HEAD_DIM, SCAN_LANES), 0)
    lane_i = lax.broadcasted_iota(jnp.int32, (HEAD_DIM, SCAN_LANES), 1)
    diag = row_i == (lane_i & (HEAD_DIM - 1))
    n_groups = RWKV_WIDTH // SCAN_LANES

    def tile_body(tt, carry):
        base = pl.multiple_of(tt * 8, 8)
        for b in range(ab):
            for hg in range(n_groups):
                ls = slice(SCAN_LANES * hg, SCAN_LANES * (hg + 1))
                rr = r_ref[b, pl.ds(base, 8), ls]
                dd = jnp.exp(d_ref[b, pl.ds(base, 8), ls])
                kk = k_ref[b, pl.ds(base, 8), ls]
                vv = v_ref[b, pl.ds(base, 8), ls]
                aa = a_ref[b, pl.ds(base, 8), ls]
                bb = b_ref[b, pl.ds(base, 8), ls]
                s = s_scr[b, hg]
                rows = []
                for i in range(8):
                    v_col = _split_sum(jnp.where(diag, vv[i:i + 1, :], 0.0), ones)
                    sa = _split_sum(s * aa[i:i + 1, :], ones)
                    s = s * dd[i:i + 1, :] + sa * bb[i:i + 1, :] + v_col * kk[i:i + 1, :]
                    o_b = _split_sum(s * rr[i:i + 1, :], ones)
                    rows.append(jnp.sum(jnp.where(diag, o_b, 0.0), axis=0, keepdims=True))
                s_scr[b, hg] = s
                o_ref[b, pl.ds(base, 8), ls] = jnp.concatenate(rows, axis=0)
        return carry

    lax.fori_loop(0, tb // 8, tile_body, 0)

    @pl.when(t_id == pl.num_programs(1) - 1)
    def _():
        sfin_ref[...] = s_scr[...]


def _scan(r, d, k, v, a, b, s0, ones, *, ab, tb):
    a_n, t_n, w = r.shape
    n_groups = w // SCAN_LANES
    grid = (a_n // ab, t_n // tb)
    tok = pl.BlockSpec((ab, tb, w), lambda i, t: (i, t, 0))
    st = pl.BlockSpec((ab, n_groups, HEAD_DIM, SCAN_LANES), lambda i, t: (i, 0, 0, 0))
    return pl.pallas_call(
        functools.partial(_scan_kernel, ab=ab, tb=tb),
        grid=grid,
        in_specs=[tok] * 6 + [st, _const_spec((SCAN_LANES, SCAN_LANES))],
        out_specs=[tok, st],
        out_shape=[jax.ShapeDtypeStruct(r.shape, F32), jax.ShapeDtypeStruct(s0.shape, F32)],
        scratch_shapes=[pltpu.VMEM((ab, n_groups, HEAD_DIM, SCAN_LANES), F32)],
        compiler_params=_cparams(("arbitrary", "arbitrary")),
        name="wkv_scan",
    )(r, d, k, v, a, b, s0, ones)


_NN = (((1,), (0,)), ((), ()))
_NT = (((1,), (1,)), ((), ()))
_TN = (((0,), (0,)), ((), ()))


def _dg(a, b, dn):
    return lax.dot_general(a, b, dn, preferred_element_type=F32)


def _split(x):
    hi = x.astype(BF16)
    return hi, (x - hi.astype(F32)).astype(BF16)


def _dot3(a, b, dn=_NN):
    ah, al = _split(a)
    bh, bl = _split(b)
    return _dg(ah, bh, dn) + _dg(ah, bl, dn) + _dg(al, bh, dn)


def _dot1(a, b, dn=_NN):
    return _dg(a.astype(BF16), b.astype(BF16), dn)


def _chunk_scan_kernel(r_ref, w_ref, k_ref, v_ref, a_ref, b_ref, s0_ref, o_ref, sfin_ref, s_scr):
    c_id = pl.program_id(1)

    @pl.when(c_id == 0)
    def _():
        s_scr[...] = s0_ref[0]

    cc = SCAN_CHUNK
    row = lax.broadcasted_iota(jnp.int32, (cc, cc), 0)
    col = lax.broadcasted_iota(jnp.int32, (cc, cc), 1)
    tri_s = row > col
    tri_i = row >= col
    ltri = jnp.where(tri_i, 1.0, 0.0).astype(BF16)

    w = w_ref[0]
    wh, wl = _split(w)
    c_in = _dg(ltri, wh, _NN) + _dg(ltri, wl, _NN)
    c_end = c_in[cc - 1:cc, :]
    e_in = jnp.exp(c_in)
    e_ex = jnp.exp(c_in - w)
    e_neg = jnp.exp(-c_in)
    e_tail = jnp.exp(c_end - c_in)
    p_end = jnp.exp(c_end)
    a_all = a_ref[0]
    b_all = b_ref[0]
    k_all = k_ref[0]
    am_all = a_all * e_ex
    bp_all = b_all * e_neg
    kp_all = k_all * e_neg
    ro_all = r_ref[0] * e_in
    bh_all = b_all * e_tail
    kh_all = k_all * e_tail

    first = lax.broadcasted_iota(jnp.int32, (cc, PAIR_LANES), 1) < HEAD_DIM
    bd = ((lax.broadcasted_iota(jnp.int32, (PAIR_LANES, PAIR_LANES), 0) >= HEAD_DIM)
          == (lax.broadcasted_iota(jnp.int32, (PAIR_LANES, PAIR_LANES), 1) >= HEAD_DIM))
    n_doubling = cc.bit_length() - 1

    for pr in range(RWKV_WIDTH // PAIR_LANES):
        ls = slice(PAIR_LANES * pr, PAIR_LANES * (pr + 1))
        s = s_scr[pr]
        am_p, bp_p, kp_p, ro_p = am_all[:, ls], bp_all[:, ls], kp_all[:, ls], ro_all[:, ls]
        v_p = v_ref[0, :, ls]
        sa = _dot3(am_p, s, _NT)
        sr = _dot1(ro_p, s, _NT)
        u_h, o_h = [], []
        for hh in range(2):
            hm = first if hh == 0 else jnp.logical_not(first)
            am_h = jnp.where(hm, am_p, 0.0)
            ro_h = jnp.where(hm, ro_p, 0.0)
            n_k = jnp.where(tri_s, _dot3(am_h, bp_p, _NT), 0.0)
            m_ak = jnp.where(tri_s, _dot3(am_h, kp_p, _NT), 0.0)
            m_rb = jnp.where(tri_i, _dot1(ro_h, bp_p, _NT), 0.0)
            m_rk = jnp.where(tri_i, _dot1(ro_h, kp_p, _NT), 0.0)
            x = sa + _dot3(m_ak, v_p)
            for kk in range(n_doubling):
                nh, nl = _split(n_k)
                xh, xl = _split(x)
                x = x + (_dg(nh, xh, _NN) + _dg(nh, xl, _NN) + _dg(nl, xh, _NN))
                if kk + 1 < n_doubling:
                    n_k = _dg(nh, nh, _NN) + _dg(nh, nl, _NN) + _dg(nl, nh, _NN)
            u_h.append(x)
            o_h.append(sr + _dot1(m_rb, x) + _dot1(m_rk, v_p))
        u_p = jnp.where(first, u_h[0], u_h[1])
        o_ref[0, :, ls] = jnp.where(first, o_h[0], o_h[1])
        uv = jnp.concatenate([u_p, v_p], axis=0)
        bk = jnp.concatenate([bh_all[:, ls], kh_all[:, ls]], axis=0)
        s_scr[pr] = jnp.where(bd, p_end[:, ls] * s + _dot3(uv, bk, _TN), 0.0)

    @pl.when(c_id == pl.num_programs(1) - 1)
    def _():
        sfin_ref[0] = s_scr[...]


def _chunk_scan(r, w, k, v, a, b, s0_bd):
    a_n, t_n, wd = r.shape
    n_pairs = wd // PAIR_LANES
    grid = (a_n, t_n // SCAN_CHUNK)
    tok = pl.BlockSpec((1, SCAN_CHUNK, wd), lambda i, t: (i, t, 0))
    st = pl.BlockSpec((1, n_pairs, PAIR_LANES, PAIR_LANES), lambda i, t: (i, 0, 0, 0))
    return pl.pallas_call(
        _chunk_scan_kernel,
        grid=grid,
        in_specs=[tok] * 6 + [st],
        out_specs=[tok, st],
        out_shape=[jax.ShapeDtypeStruct(r.shape, F32), jax.ShapeDtypeStruct(s0_bd.shape, F32)],
        scratch_shapes=[pltpu.VMEM((n_pairs, PAIR_LANES, PAIR_LANES), F32)],
        compiler_params=_cparams(("arbitrary", "arbitrary")),
        name="wkv_chunk_scan",
    )(r, w, k, v, a, b, s0_bd)


def _state_to_blockdiag(s):
    a_n = s.shape[0]
    s = s.reshape(a_n, N_RWKV_HEADS // 2, 2, HEAD_DIM, HEAD_DIM)
    z = jnp.zeros_like(s[:, :, 0])
    top = jnp.concatenate([s[:, :, 0], z], axis=-1)
    bot = jnp.concatenate([z, s[:, :, 1]], axis=-1)
    return jnp.concatenate([top, bot], axis=-2)


def _state_from_blockdiag(s):
    a_n = s.shape[0]
    s = s.reshape(a_n, N_RWKV_HEADS // 2, 2, HEAD_DIM, 2, HEAD_DIM)
    return jnp.stack([s[:, :, 0, :, 0, :], s[:, :, 1, :, 1, :]], axis=2).reshape(
        a_n, N_RWKV_HEADS, HEAD_DIM, HEAD_DIM)


def _attn_kernel(sink_ref, q_ref, kp_ref, vp_ref, kc_ref, vc_ref, o_ref, *, sb, tq, block_prev):
    rows = GROUP * tq
    assert tq & (tq - 1) == 0
    t_row = lax.broadcasted_iota(jnp.int32, (rows, WINDOW), 0) & (tq - 1)
    c_prev = lax.broadcasted_iota(jnp.int32, (rows, WINDOW), 1)
    if block_prev:
        t_row = t_row + jnp.where(pl.program_id(1) >= 1, 0, WINDOW)
    prev_mask = c_prev > t_row
    t_row_c = lax.broadcasted_iota(jnp.int32, (rows, tq), 0) & (tq - 1)
    c_cur = lax.broadcasted_iota(jnp.int32, (rows, tq), 1)
    cur_mask = c_cur <= t_row_c
    grp = lax.broadcasted_iota(jnp.int32, (rows, 1), 0) >> (tq.bit_length() - 1)
    scale = HEAD_DIM ** -0.5
    for s in range(sb):
        q = q_ref[s]
        outs = []
        for h2 in range(N_KV_HEADS):
            hs = slice(HEAD_DIM * h2, HEAD_DIM * (h2 + 1))
            kp = kp_ref[s, :, hs].astype(BF16)
            vp = vp_ref[s, :, hs].astype(BF16)
            kc = kc_ref[s, :, hs].astype(BF16)
            vc = vc_ref[s, :, hs].astype(BF16)
            qh = jnp.concatenate(
                [q[:, HEAD_DIM * (h2 * GROUP + g):HEAD_DIM * (h2 * GROUP + g + 1)] for g in range(GROUP)], axis=0)
            qh = qh.astype(BF16)
            sink = jnp.zeros((rows, 1), F32)
            for g in range(GROUP):
                sink = jnp.where(grp == g, sink_ref[h2 * GROUP + g], sink)
            s_p = jnp.where(prev_mask, _dot_nt(qh, kp) * scale, NEG_INF)
            s_c = jnp.where(cur_mask, _dot_nt(qh, kc) * scale, NEG_INF)
            m = jnp.maximum(jnp.maximum(jnp.max(s_p, axis=-1, keepdims=True),
                                        jnp.max(s_c, axis=-1, keepdims=True)), sink)
            p_p = jnp.exp(s_p - m)
            p_c = jnp.exp(s_c - m)
            den = (jnp.sum(p_p, axis=-1, keepdims=True) + jnp.sum(p_c, axis=-1, keepdims=True)
                   + jnp.exp(sink - m))
            inv = 1.0 / den
            o = _dot((p_p * inv).astype(BF16), vp) + _dot((p_c * inv).astype(BF16), vc)
            outs.extend(o[g * tq:(g + 1) * tq, :] for g in range(GROUP))
        o_ref[s] = jnp.concatenate(outs, axis=-1)


def _attn(sinks, q, k_prev, v_prev, k_cur, v_cur, *, sb, tq, block_prev):
    a_n, t_n, _ = q.shape
    grid = (a_n // sb, t_n // tq)
    cur = lambda c: pl.BlockSpec((sb, tq, c), lambda a, t: (a, t, 0))
    if block_prev:
        prev = pl.BlockSpec((sb, WINDOW, KV_WIDTH), lambda a, t: (a, jnp.maximum(t - 1, 0), 0))
    else:
        prev = pl.BlockSpec((sb, WINDOW, KV_WIDTH), lambda a, t: (a, 0, 0))
    return pl.pallas_call(
        functools.partial(_attn_kernel, sb=sb, tq=tq, block_prev=block_prev),
        grid=grid,
        in_specs=[pl.BlockSpec(memory_space=pltpu.SMEM), cur(ATT_WIDTH), prev, prev, cur(KV_WIDTH), cur(KV_WIDTH)],
        out_specs=cur(ATT_WIDTH),
        out_shape=jax.ShapeDtypeStruct(q.shape, F32),
        compiler_params=_cparams(("arbitrary", "arbitrary")),
        name="swa_attn",
    )(sinks, q, k_prev, v_prev, k_cur, v_cur)


def _mix_out_kernel(x_ref, gt_ref, o_ref, bonus_ref, g_ref, att_ref, lnw_ref, lnb_ref, ones_ref, wo_ref, y_ref):
    x3 = x_ref[...]
    ab, tb, _ = x3.shape
    tm = ab * tb
    ones = ones_ref[...]
    o = o_ref[...].reshape(tm, RWKV_WIDTH)
    mu = _split_sum(o, ones) * (1.0 / HEAD_DIM)
    oc = o - mu
    var = _split_sum(oc * oc, ones) * (1.0 / HEAD_DIM)
    on = oc * lax.rsqrt(var + LNX_EPS) * lnw_ref[...] + lnb_ref[...]
    rw = (on + bonus_ref[...].reshape(tm, RWKV_WIDTH)) * g_ref[...].reshape(tm, RWKV_WIDTH)
    att = att_ref[...].reshape(tm, ATT_WIDTH)
    m = _dot(rw.astype(BF16), wo_ref[:RWKV_WIDTH, :]) + _dot(att.astype(BF16), wo_ref[RWKV_WIDTH:, :])
    y_ref[...] = x3 + gt_ref[...] * m.reshape(ab, tb, D_MODEL)


def _mix_out(x, mods, o, bonus, g, att, p, *, ab, tb):
    a_n, t_n, _ = x.shape
    grid = (a_n // ab, t_n // tb)
    tok = lambda c: pl.BlockSpec((ab, tb, c), lambda a, t: (a, t, 0))
    w = RWKV_WIDTH
    return pl.pallas_call(
        _mix_out_kernel,
        grid=grid,
        in_specs=[
            tok(D_MODEL),
            pl.BlockSpec((ab, 1, D_MODEL), lambda a, t: (a, 0, 5)),
            tok(w), tok(w), tok(w), tok(ATT_WIDTH),
            _const_spec((1, w)), _const_spec((1, w)), _const_spec((w, w)),
            _const_spec((w + ATT_WIDTH, D_MODEL)),
        ],
        out_specs=tok(D_MODEL),
        out_shape=jax.ShapeDtypeStruct(x.shape, F32),
        compiler_params=_cparams(("arbitrary", "arbitrary")),
        name="mix_out",
    )(x, mods, o, bonus, g, att, p["ln_x_w"], p["ln_x_b"], p["head_ones"], p["w_out"])


def _rope_tables(pos, width):
    half = ROPE_DIM // 2
    inv_freq = ROPE_THETA ** (-jnp.arange(0, ROPE_DIM, 2, dtype=F32) / ROPE_DIM)
    ang = pos.astype(F32)[:, None] * inv_freq[None, :]
    cos, sin = jnp.cos(ang), jnp.sin(ang)
    t = pos.shape[0]
    pad = jnp.zeros((t, HEAD_DIM - ROPE_DIM), F32)
    zero = jnp.zeros((t, half), F32)
    c_head = jnp.concatenate([cos, cos, pad + 1.0], axis=-1)
    sa_head = jnp.concatenate([zero, sin, pad], axis=-1)
    sb_head = jnp.concatenate([-sin, zero, pad], axis=-1)
    reps = width // HEAD_DIM
    tile = lambda x: jnp.tile(x, (1, reps))[None]
    return tile(c_head), tile(sa_head), tile(sb_head)


def _state_to_lanes(s):
    a_n = s.shape[0]
    s = s.reshape(a_n, N_RWKV_HEADS // SCAN_HEADS, SCAN_HEADS, HEAD_DIM, HEAD_DIM)
    return s.transpose(0, 1, 3, 2, 4).reshape(a_n, N_RWKV_HEADS // SCAN_HEADS, HEAD_DIM, SCAN_LANES)


def _state_from_lanes(s):
    a_n = s.shape[0]
    s = s.reshape(a_n, N_RWKV_HEADS // SCAN_HEADS, HEAD_DIM, SCAN_HEADS, HEAD_DIM)
    return s.transpose(0, 1, 3, 2, 4).reshape(a_n, N_RWKV_HEADS, HEAD_DIM, HEAD_DIM)


def _layer(x, mods, s0, u_prev0, k_buf, v_buf, pos0, p, *, ab, tb, scan_ab, scan_tb, attn_sb, final_norm_w):
    a_n, t_n, _ = x.shape
    x = _ffn(x, mods, p["norm_ffn1"], p["ffn1_w_gu"], p["ffn1_w_down"], final_norm_w,
             mod_base=0, ab=ab, tb=tb, final_norm=False)
    pos = pos0 + jnp.arange(t_n)
    rope = _rope_tables(pos, ATT_WIDTH) + _rope_tables(pos, KV_WIDTH)
    mix_ab, mix_tb = (ab // 2, tb) if ab > 1 else (ab, tb // 2)
    (r, d, k, v, a, b, g, bonus, q, k_att, v_att, u_last) = _mix_in(x, mods, u_prev0, p, rope,
                                                                   ab=mix_ab, tb=mix_tb)
    if t_n % SCAN_CHUNK == 0:
        o, s_fin = _chunk_scan(r, d, k, v, a, b, _state_to_blockdiag(s0))
        s_fin = _state_from_blockdiag(s_fin)
    else:
        o, s_fin = _scan(r, d, k, v, a, b, _state_to_lanes(s0), p["scan_ones"], ab=scan_ab, tb=scan_tb)
        s_fin = _state_from_lanes(s_fin)
    if k_buf is None:
        att = _attn(p["attn_sinks"], q, k_att, v_att, k_att, v_att, sb=1, tq=WINDOW, block_prev=True)
        k_new, v_new = k_att[:, -WINDOW:], v_att[:, -WINDOW:]
    else:
        kb = k_buf.reshape(a_n, WINDOW, KV_WIDTH)
        vb = v_buf.reshape(a_n, WINDOW, KV_WIDTH)
        att = _attn(p["attn_sinks"], q, kb, vb, k_att, v_att, sb=attn_sb, tq=t_n, block_prev=False)
        k_new = jnp.concatenate([kb, k_att], axis=1)[:, -WINDOW:]
        v_new = jnp.concatenate([vb, v_att], axis=1)[:, -WINDOW:]
    x = _mix_out(x, mods, o, bonus, g, att, p, ab=ab, tb=tb)
    y = _ffn(x, mods, p["norm_ffn2"], p["ffn2_w_gu"], p["ffn2_w_down"], final_norm_w,
             mod_base=6, ab=ab, tb=tb, final_norm=True)
    kv_shape = (a_n, WINDOW, N_KV_HEADS, HEAD_DIM)
    return y, (s_fin, u_last[:, 0, :], k_new.reshape(kv_shape), v_new.reshape(kv_shape))


def kernel(x_prompt, x_sample, c_prompt, c_sample, state_wkv, state_shift, cache_k, cache_v, w_ada, b_ada,
           norm_ffn1, ffn1_w_gu, ffn1_w_down, norm_mix, w_in, mu_shift, rwkv_w0, rwkv_w2, rwkv_a0, rwkv_a2,
           rwkv_g2, rwkv_k_k, rwkv_k_a, rwkv_r_k, ln_x_w, ln_x_b, attn_sinks, w_out, norm_ffn2, ffn2_w_gu,
           ffn2_w_down, norm_final):
    depth = w_ada.shape[0]
    assert depth == 1, "final norm is fused into the last layer's second FFN"
    bp = x_prompt.shape[0]
    bs = x_sample.shape[0]
    w = RWKV_WIDTH
    lane_head = jnp.arange(w) // HEAD_DIM
    head_ones = (lane_head[:, None] == lane_head[None, :]).astype(BF16)
    fnw = norm_final.reshape(1, 1, D_MODEL)

    n_c = bp + bs
    n_pad = -n_c % 8
    c_all = jnp.concatenate([c_prompt, c_sample, jnp.zeros((n_pad, D_MODEL), F32)], axis=0)

    l = 0
    p = {
        "norm_ffn1": norm_ffn1[l].reshape(1, 1, D_MODEL),
        "ffn1_w_gu": ffn1_w_gu[l].astype(BF16), "ffn1_w_down": ffn1_w_down[l].astype(BF16),
        "norm_mix": norm_mix[l].reshape(1, 1, D_MODEL),
        "w_in": w_in[l].astype(BF16),
        "mu_shift": mu_shift[l].reshape(1, 1, RWKV_COLS),
        "rwkv_w0": rwkv_w0[l].reshape(1, w), "rwkv_w2": rwkv_w2[l].astype(BF16),
        "rwkv_a0": rwkv_a0[l].reshape(1, w), "rwkv_a2": rwkv_a2[l].astype(BF16),
        "rwkv_g2": rwkv_g2[l].astype(BF16),
        "rwkv_k_k": rwkv_k_k[l].reshape(1, w), "rwkv_k_a": rwkv_k_a[l].reshape(1, w),
        "rwkv_r_k": rwkv_r_k[l].reshape(1, w),
        "ln_x_w": ln_x_w[l].reshape(1, w), "ln_x_b": ln_x_b[l].reshape(1, w),
        "attn_sinks": attn_sinks[l],
        "w_out": w_out[l].astype(BF16),
        "norm_ffn2": norm_ffn2[l].reshape(1, 1, D_MODEL),
        "ffn2_w_gu": ffn2_w_gu[l].astype(BF16), "ffn2_w_down": ffn2_w_down[l].astype(BF16),
        "head_ones": head_ones,
        "scan_ones": head_ones[:SCAN_LANES, :SCAN_LANES],
    }
    mods_all = _ada(c_all, w_ada[l], b_ada[l])
    mods_p = mods_all[:bp].reshape(bp, 1, N_ADA * D_MODEL)
    mods_s = mods_all[bp:n_c].reshape(bs, 1, N_ADA * D_MODEL)

    s0_p = jnp.zeros((bp, N_RWKV_HEADS, HEAD_DIM, HEAD_DIM), F32)
    u0_p = jnp.zeros((bp, 1, RWKV_COLS), F32)
    yp, st_p = _layer(x_prompt, mods_p, s0_p, u0_p, None, None, 0, p,
                      ab=1, tb=512, scan_ab=bp, scan_tb=128, attn_sb=1, final_norm_w=fnw)
    t_s = x_sample.shape[1]
    ys, st_s = _layer(x_sample, mods_s, state_wkv[l], state_shift[l].reshape(bs, 1, RWKV_COLS),
                      cache_k[l], cache_v[l], PAST_LEN, p,
                      ab=512 // t_s, tb=t_s, scan_ab=4, scan_tb=t_s, attn_sb=8, final_norm_w=fnw)
    return (yp, ys, st_p[0][None], st_p[1][None], st_p[2][None], st_p[3][None],
            st_s[0][None], st_s[1][None], st_s[2][None], st_s[3][None])
```

```python
import functools

import jax
import jax.numpy as jnp
from jax import lax
from jax.experimental import pallas as pl
from jax.experimental.pallas import tpu as pltpu

F32 = jnp.float32
BF16 = jnp.bfloat16

D_MODEL = 1024
HEAD_DIM = 64
N_RWKV_HEADS = 8
RWKV_WIDTH = N_RWKV_HEADS * HEAD_DIM
N_Q_HEADS = 8
N_KV_HEADS = 2
GROUP = N_Q_HEADS // N_KV_HEADS
ATT_WIDTH = N_Q_HEADS * HEAD_DIM
KV_WIDTH = N_KV_HEADS * HEAD_DIM
DECAY_LORA = 64
AAA_LORA = 64
GATE_LORA = 128
RWKV_COLS = 3 * RWKV_WIDTH + DECAY_LORA + AAA_LORA + GATE_LORA
IN_COLS = RWKV_COLS + ATT_WIDTH + 2 * KV_WIDTH
WINDOW = 128
ROPE_THETA = 500000.0
ROPE_DIM = HEAD_DIM // 4
D_FF = 2816
N_ADA = 9
NORM_EPS = 1e-5
LNX_EPS = 64e-5
NEG_INF = -1e30
PAST_LEN = 16384

V7X_VMEM_LIMIT_BYTES = 56 * 1024 * 1024
FFN_CHUNK = 256
SCAN_LANES = 256
SCAN_HEADS = SCAN_LANES // HEAD_DIM
SCAN_CHUNK = 64
PAIR_LANES = 2 * HEAD_DIM
CHUNK_SCAN_SEQS = 4


def _cparams(sem):
    return pltpu.CompilerParams(dimension_semantics=sem, vmem_limit_bytes=V7X_VMEM_LIMIT_BYTES)


def _const_spec(shape):
    nd = len(shape)
    return pl.BlockSpec(shape, lambda *_: (0,) * nd)


def _dot(a, b):
    return jnp.dot(a, b, preferred_element_type=F32)


def _dot_nt(a, b):
    return lax.dot_general(a, b, (((1,), (1,)), ((), ())), preferred_element_type=F32)


def _split_sum(x, ones):
    hi = x.astype(BF16)
    lo = (x - hi.astype(F32)).astype(BF16)
    return _dot(hi, ones) + _dot(lo, ones)


def _mod_norm(x3, nw, sc, sh):
    ms = jnp.mean(x3 * x3, axis=-1, keepdims=True)
    y = x3 * lax.rsqrt(ms + NORM_EPS) * nw
    return y * (1.0 + sc) + sh


def _ada_kernel(c_ref, w_ref, b_ref, o_ref):
    c = c_ref[...]
    s = (c * jax.nn.sigmoid(c)).astype(BF16)
    o_ref[...] = _dot(s, w_ref[...].astype(BF16)) + b_ref[...]


def _ada(c_all, w_ada, b_ada):
    n = c_all.shape[0]
    cols = N_ADA * D_MODEL
    return pl.pallas_call(
        _ada_kernel,
        grid=(N_ADA,),
        in_specs=[
            pl.BlockSpec((n, D_MODEL), lambda i: (0, 0)),
            pl.BlockSpec((D_MODEL, D_MODEL), lambda i: (0, i)),
            pl.BlockSpec((1, D_MODEL), lambda i: (0, i)),
        ],
        out_specs=pl.BlockSpec((n, D_MODEL), lambda i: (0, i)),
        out_shape=jax.ShapeDtypeStruct((n, cols), F32),
        compiler_params=_cparams(("arbitrary",)),
        name="ada",
    )(c_all, w_ada, b_ada.reshape(1, cols))


def _ffn_kernel(x_ref, sh_ref, sc_ref, gt_ref, nw_ref, wgu_ref, wd_ref, fn_ref, o_ref, h_scr, acc_scr,
                *, final_norm):
    x3 = x_ref[...]
    ab, tb, _ = x3.shape
    h3 = _mod_norm(x3, nw_ref[...], sc_ref[...], sh_ref[...])
    h_scr[...] = h3.reshape(ab * tb, D_MODEL).astype(BF16)
    for c in range(D_FF // FFN_CHUNK):
        lo = c * FFN_CHUNK
        h = h_scr[...]
        g = _dot(h, wgu_ref[:, lo:lo + FFN_CHUNK])
        u = _dot(h, wgu_ref[:, D_FF + lo:D_FF + lo + FFN_CHUNK])
        act = (g * jax.nn.sigmoid(g) * u).astype(BF16)
        part = _dot(act, wd_ref[lo:lo + FFN_CHUNK, :])
        if c == 0:
            acc_scr[...] = part
        else:
            acc_scr[...] += part
    y3 = x3 + (0.5 * gt_ref[...]) * acc_scr[...].reshape(ab, tb, D_MODEL)
    if final_norm:
        ms = jnp.mean(y3 * y3, axis=-1, keepdims=True)
        y3 = y3 * lax.rsqrt(ms + NORM_EPS) * fn_ref[...]
    o_ref[...] = y3


def _ffn(x, mods, nw, wgu, wd, fnw, *, mod_base, ab, tb, final_norm):
    a_n, t_n, _ = x.shape
    grid = (a_n // ab, t_n // tb)
    mod_spec = lambda k: pl.BlockSpec((ab, 1, D_MODEL), lambda a, t, k=k: (a, 0, mod_base + k))
    return pl.pallas_call(
        functools.partial(_ffn_kernel, final_norm=final_norm),
        grid=grid,
        in_specs=[
            pl.BlockSpec((ab, tb, D_MODEL), lambda a, t: (a, t, 0)),
            mod_spec(0), mod_spec(1), mod_spec(2),
            _const_spec((1, 1, D_MODEL)),
            pl.BlockSpec((D_MODEL, 2 * D_FF), lambda a, t: (0, 0), pipeline_mode=pl.Buffered(1)),
            pl.BlockSpec((D_FF, D_MODEL), lambda a, t: (0, 0), pipeline_mode=pl.Buffered(1)),
            _const_spec((1, 1, D_MODEL)),
        ],
        out_specs=pl.BlockSpec((ab, tb, D_MODEL), lambda a, t: (a, t, 0)),
        out_shape=jax.ShapeDtypeStruct(x.shape, F32),
        scratch_shapes=[pltpu.VMEM((ab * tb, D_MODEL), BF16), pltpu.VMEM((ab * tb, D_MODEL), F32)],
        compiler_params=_cparams(("arbitrary", "arbitrary")),
        name="ffn_final" if final_norm else "ffn",
    )(x, mods, mods, mods, nw, wgu, wd, fnw)


def _mix_in_kernel(x_ref, sh_ref, sc_ref, nw_ref, win_ref, uprev_ref, mu_ref, w0_ref, w2_ref, a0_ref, a2_ref,
                   g2_ref, kk_ref, ka_ref, rk_ref, ones_ref, cq_ref, sqa_ref, sqb_ref, ck_ref, ska_ref, skb_ref,
                   r_out, d_out, k_out, v_out, a_out, b_out, g_out, bonus_out, q_out, ka_out, va_out, ulast_out,
                   carry_scr):
    t_id = pl.program_id(1)
    x3 = x_ref[...]
    ab, tb, _ = x3.shape
    tm = ab * tb
    h = _mod_norm(x3, nw_ref[...], sc_ref[...], sh_ref[...]).reshape(tm, D_MODEL).astype(BF16)
    proj = _dot(h, win_ref[...])

    u3 = proj[:, :RWKV_COLS].reshape(ab, tb, RWKV_COLS)

    @pl.when(t_id == 0)
    def _():
        carry_scr[...] = uprev_ref[...]

    prev = carry_scr[...]
    t_idx = lax.broadcasted_iota(jnp.int32, (ab, tb, RWKV_COLS), 1)
    u_shift = jnp.where(t_idx == 0, prev, pltpu.roll(u3, 1, 1))
    last = u3[:, tb - 1:tb, :]
    carry_scr[...] = last
    ulast_out[...] = last

    z = (u3 + (u_shift - u3) * mu_ref[...]).reshape(tm, RWKV_COLS)
    w = RWKV_WIDTH
    r = z[:, :w]
    k = z[:, w:2 * w]
    v = z[:, 2 * w:3 * w]
    zw = z[:, 3 * w:3 * w + DECAY_LORA]
    za = z[:, 3 * w + DECAY_LORA:3 * w + DECAY_LORA + AAA_LORA]
    zg = z[:, 3 * w + DECAY_LORA + AAA_LORA:]

    y = -(w0_ref[...] + _dot(jnp.tanh(zw).astype(BF16), w2_ref[...]))
    softplus = jnp.maximum(y, 0.0) + jnp.log1p(jnp.exp(-jnp.abs(y)))
    w_log = -softplus - 0.5
    log_decay = -jnp.exp(w_log)
    a = jax.nn.sigmoid(a0_ref[...] + _dot(za.astype(BF16), a2_ref[...]))
    g = _dot(jax.nn.sigmoid(zg).astype(BF16), g2_ref[...])

    ones = ones_ref[...]
    kk = k * kk_ref[...]
    kk = kk * lax.rsqrt(jnp.maximum(_split_sum(kk * kk, ones), 1e-24))
    k_mod = k * (1.0 + (a - 1.0) * ka_ref[...])
    bonus = _split_sum(r * k_mod * rk_ref[...], ones) * v

    shp = (ab, tb, w)
    r_out[...] = r.reshape(shp)
    d_out[...] = log_decay.reshape(shp)
    k_out[...] = k_mod.reshape(shp)
    v_out[...] = v.reshape(shp)
    a_out[...] = (-kk).reshape(shp)
    b_out[...] = (kk * a).reshape(shp)
    g_out[...] = g.reshape(shp)
    bonus_out[...] = bonus.reshape(shp)

    q3 = proj[:, RWKV_COLS:RWKV_COLS + ATT_WIDTH].reshape(ab, tb, ATT_WIDTH)
    q_out[...] = (q3 * cq_ref[...] + pltpu.roll(q3, ROPE_DIM // 2, 2) * sqa_ref[...]
                  + pltpu.roll(q3, ATT_WIDTH - ROPE_DIM // 2, 2) * sqb_ref[...])
    k3 = proj[:, RWKV_COLS + ATT_WIDTH:RWKV_COLS + ATT_WIDTH + KV_WIDTH].reshape(ab, tb, KV_WIDTH)
    ka_out[...] = (k3 * ck_ref[...] + pltpu.roll(k3, ROPE_DIM // 2, 2) * ska_ref[...]
                   + pltpu.roll(k3, KV_WIDTH - ROPE_DIM // 2, 2) * skb_ref[...])
    va_out[...] = proj[:, RWKV_COLS + ATT_WIDTH + KV_WIDTH:].reshape(ab, tb, KV_WIDTH)


def _mix_in(x, mods, u_prev0, p, rope, *, ab, tb):
    a_n, t_n, _ = x.shape
    grid = (a_n // ab, t_n // tb)
    tok = lambda c: pl.BlockSpec((ab, tb, c), lambda a, t: (a, t, 0))
    seq = lambda c: pl.BlockSpec((ab, 1, c), lambda a, t: (a, 0, 0))
    mod_spec = lambda k: pl.BlockSpec((ab, 1, D_MODEL), lambda a, t, k=k: (a, 0, k))
    rope_spec = lambda c: pl.BlockSpec((1, tb, c), lambda a, t: (0, t, 0))
    w = RWKV_WIDTH
    tok_shape = lambda c: jax.ShapeDtypeStruct((a_n, t_n, c), F32)
    return pl.pallas_call(
        _mix_in_kernel,
        grid=grid,
        in_specs=[
            tok(D_MODEL), mod_spec(3), mod_spec(4),
            _const_spec((1, 1, D_MODEL)),
            pl.BlockSpec((D_MODEL, IN_COLS), lambda a, t: (0, 0), pipeline_mode=pl.Buffered(1)),
            seq(RWKV_COLS),
            _const_spec((1, 1, RWKV_COLS)),
            _const_spec((1, w)), _const_spec((DECAY_LORA, w)),
            _const_spec((1, w)), _const_spec((AAA_LORA, w)),
            _const_spec((GATE_LORA, w)),
            _const_spec((1, w)), _const_spec((1, w)), _const_spec((1, w)),
            _const_spec((w, w)),
            rope_spec(ATT_WIDTH), rope_spec(ATT_WIDTH), rope_spec(ATT_WIDTH),
            rope_spec(KV_WIDTH), rope_spec(KV_WIDTH), rope_spec(KV_WIDTH),
        ],
        out_specs=[tok(w)] * 8 + [tok(ATT_WIDTH), tok(KV_WIDTH), tok(KV_WIDTH), seq(RWKV_COLS)],
        out_shape=[tok_shape(w)] * 8 + [tok_shape(ATT_WIDTH), tok_shape(KV_WIDTH), tok_shape(KV_WIDTH),
                                        jax.ShapeDtypeStruct((a_n, 1, RWKV_COLS), F32)],
        scratch_shapes=[pltpu.VMEM((ab, 1, RWKV_COLS), F32)],
        compiler_params=_cparams(("arbitrary", "arbitrary")),
        name="mix_in",
    )(x, mods, mods, p["norm_mix"], p["w_in"], u_prev0, p["mu_shift"], p["rwkv_w0"], p["rwkv_w2"],
      p["rwkv_a0"], p["rwkv_a2"], p["rwkv_g2"], p["rwkv_k_k"], p["rwkv_k_a"], p["rwkv_r_k"], p["head_ones"],
      *rope)


def _scan_kernel(r_ref, d_ref, k_ref, v_ref, a_ref, b_ref, s0_ref, ones_ref, o_ref, sfin_ref, s_scr, *, ab, tb):
    t_id = pl.program_id(1)

    @pl.when(t_id == 0)
    def _():
        s_scr[...] = s0_ref[...]

    ones = ones_ref[...]
    row_i = lax.broadcasted_iota(jnp.int32, (HEAD_DIM, SCAN_LANES), 0)
    lane_i = lax.broadcasted_iota(jnp.int32, (HEAD_DIM, SCAN_LANES), 1)
    diag = row_i == (lane_i & (HEAD_DIM - 1))
    n_groups = RWKV_WIDTH // SCAN_LANES

    def tile_body(tt, carry):
        base = pl.multiple_of(tt * 8, 8)
        for b in range(ab):
            for hg in range(n_groups):
                ls = slice(SCAN_LANES * hg, SCAN_LANES * (hg + 1))
                rr = r_ref[b, pl.ds(base, 8), ls]
                dd = jnp.exp(d_ref[b, pl.ds(base, 8), ls])
                kk = k_ref[b, pl.ds(base, 8), ls]
                vv = v_ref[b, pl.ds(base, 8), ls]
                aa = a_ref[b, pl.ds(base, 8), ls]
                bb = b_ref[b, pl.ds(base, 8), ls]
                s = s_scr[b, hg]
                rows = []
                for i in range(8):
                    v_col = _split_sum(jnp.where(diag, vv[i:i + 1, :], 0.0), ones)
                    sa = _split_sum(s * aa[i:i + 1, :], ones)
                    s = s * dd[i:i + 1, :] + sa * bb[i:i + 1, :] + v_col * kk[i:i + 1, :]
                    o_b = _split_sum(s * rr[i:i + 1, :], ones)
                    rows.append(jnp.sum(jnp.where(diag, o_b, 0.0), axis=0, keepdims=True))
                s_scr[b, hg] = s
                o_ref[b, pl.ds(base, 8), ls] = jnp.concatenate(rows, axis=0)
        return carry

    lax.fori_loop(0, tb // 8, tile_body, 0)

    @pl.when(t_id == pl.num_programs(1) - 1)
    def _():
        sfin_ref[...] = s_scr[...]


def _scan(r, d, k, v, a, b, s0, ones, *, ab, tb):
    a_n, t_n, w = r.shape
    n_groups = w // SCAN_LANES
    grid = (a_n // ab, t_n // tb)
    tok = pl.BlockSpec((ab, tb, w), lambda i, t: (i, t, 0))
    st = pl.BlockSpec((ab, n_groups, HEAD_DIM, SCAN_LANES), lambda i, t: (i, 0, 0, 0))
    return pl.pallas_call(
        functools.partial(_scan_kernel, ab=ab, tb=tb),
        grid=grid,
        in_specs=[tok] * 6 + [st, _const_spec((SCAN_LANES, SCAN_LANES))],
        out_specs=[tok, st],
        out_shape=[jax.ShapeDtypeStruct(r.shape, F32), jax.ShapeDtypeStruct(s0.shape, F32)],
        scratch_shapes=[pltpu.VMEM((ab, n_groups, HEAD_DIM, SCAN_LANES), F32)],
        compiler_params=_cparams(("arbitrary", "arbitrary")),
        name="wkv_scan",
    )(r, d, k, v, a, b, s0, ones)


_NN = (((1,), (0,)), ((), ()))
_NT = (((1,), (1,)), ((), ()))
_TN = (((0,), (0,)), ((), ()))


def _dg(a, b, dn):
    return lax.dot_general(a, b, dn, preferred_element_type=F32)


def _split(x):
    hi = x.astype(BF16)
    return hi, (x - hi.astype(F32)).astype(BF16)


def _dot3(a, b, dn=_NN):
    ah, al = _split(a)
    bh, bl = _split(b)
    return _dg(ah, bh, dn) + _dg(ah, bl, dn) + _dg(al, bh, dn)


def _dot1(a, b, dn=_NN):
    return _dg(a.astype(BF16), b.astype(BF16), dn)


def _chunk_scan_kernel(r_ref, w_ref, k_ref, v_ref, a_ref, b_ref, s0_ref, o_ref, sfin_ref, s_scr):
    c_id = pl.program_id(1)

    @pl.when(c_id == 0)
    def _():
        s_scr[...] = s0_ref[...]

    cc = SCAN_CHUNK
    nb = r_ref.shape[0]
    n_pairs = RWKV_WIDTH // PAIR_LANES
    row = lax.broadcasted_iota(jnp.int32, (cc, cc), 0)
    col = lax.broadcasted_iota(jnp.int32, (cc, cc), 1)
    ltri = jnp.where(row >= col, 1.0, 0.0).astype(BF16)

    am_u, bp_u, kp_u, ro_u, bh_u, kh_u, pe_u, v_u = [], [], [], [], [], [], [], []
    for bb in range(nb):
        w = w_ref[bb]
        wh, wl = _split(w)
        c_in = _dg(ltri, wh, _NN) + _dg(ltri, wl, _NN)
        c_end = c_in[cc - 1:cc, :]
        e_in = jnp.exp(c_in)
        e_ex = jnp.exp(c_in - w)
        e_neg = jnp.exp(-c_in)
        e_tail = jnp.exp(c_end - c_in)
        p_end = jnp.exp(c_end)
        a_all = a_ref[bb]
        b_all = b_ref[bb]
        k_all = k_ref[bb]
        am_all = a_all * e_ex
        bp_all = b_all * e_neg
        kp_all = k_all * e_neg
        ro_all = r_ref[bb] * e_in
        bh_all = b_all * e_tail
        kh_all = k_all * e_tail
        for pr in range(n_pairs):
            ls = slice(PAIR_LANES * pr, PAIR_LANES * (pr + 1))
            am_u.append(am_all[:, ls])
            bp_u.append(bp_all[:, ls])
            kp_u.append(kp_all[:, ls])
            ro_u.append(ro_all[:, ls])
            bh_u.append(bh_all[:, ls])
            kh_u.append(kh_all[:, ls])
            pe_u.append(p_end[:, ls])
            v_u.append(v_ref[bb, :, ls])

    first = lax.broadcasted_iota(jnp.int32, (cc, PAIR_LANES), 1) < HEAD_DIM
    bd = ((lax.broadcasted_iota(jnp.int32, (PAIR_LANES, PAIR_LANES), 0) >= HEAD_DIM)
          == (lax.broadcasted_iota(jnp.int32, (PAIR_LANES, PAIR_LANES), 1) >= HEAD_DIM))
    n_doubling = cc.bit_length() - 1
    wide = 4 * cc
    row_w = lax.broadcasted_iota(jnp.int32, (2 * cc, wide), 0) & (cc - 1)
    col_w = lax.broadcasted_iota(jnp.int32, (2 * cc, wide), 1)
    tri_s_w = row_w > (col_w & (cc - 1))
    col_z = lax.broadcasted_iota(jnp.int32, (cc, wide), 1)
    z_lo = col_z >= 2 * cc
    z_n = col_z < 3 * cc
    row2 = lax.broadcasted_iota(jnp.int32, (2 * cc, 2 * cc), 0) & (cc - 1)
    col2 = lax.broadcasted_iota(jnp.int32, (2 * cc, 2 * cc), 1) & (cc - 1)
    tri_i2 = row2 >= col2

    def rows3(x):
        hi, lo = _split(x)
        return jnp.concatenate([hi, lo, hi], axis=0)

    def lanes3_lhs(x):
        hi, lo = _split(x)
        return jnp.concatenate([hi, hi, lo], axis=1)

    def lanes3_rhs(x):
        hi, lo = _split(x)
        return jnp.concatenate([hi, lo, hi], axis=1)

    def packed(z):
        hi, lo = _split(z)
        return jnp.where(z_lo, lo, hi)[:, :3 * cc], jnp.concatenate([hi, lo, hi], axis=0)

    pairs = range(nb * n_pairs)
    heads = range(2 * nb * n_pairs)
    s_old = [s_scr[pr // n_pairs, pr % n_pairs] for pr in pairs]
    s_t = [s_old[pr].T for pr in pairs]
    v_p = v_u
    sa = [_dg(lanes3_lhs(am_u[pr]), rows3(s_t[pr]), _NN) for pr in pairs]
    sr = [_dot1(ro_u[pr], s_t[pr]) for pr in pairs]

    n4, mak4, mr = [], [], []
    for pr in pairs:
        am_p = am_u[pr]
        ro_p = ro_u[pr]
        am_st = lanes3_lhs(jnp.concatenate([jnp.where(first, am_p, 0.0), jnp.where(first, 0.0, am_p)], axis=0))
        ro_st = jnp.concatenate([jnp.where(first, ro_p, 0.0), jnp.where(first, 0.0, ro_p)], axis=0).astype(BF16)
        b3 = lanes3_rhs(bp_u[pr])
        k3 = lanes3_rhs(kp_u[pr])
        bk1 = jnp.concatenate([bp_u[pr], kp_u[pr]], axis=0).astype(BF16)
        n4.append(jnp.where(tri_s_w, _dg(am_st, jnp.concatenate([b3] * 4, axis=0), _NT), 0.0))
        mak4.append(jnp.where(tri_s_w, _dg(am_st, jnp.concatenate([k3] * 4, axis=0), _NT), 0.0))
        mr.append(jnp.where(tri_i2, _dg(ro_st, bk1, _NT), 0.0).astype(BF16))

    z = []
    for h in heads:
        pr, odd = h // 2, h % 2
        rs = slice(cc * odd, cc * (odd + 1))
        v_src = v_p[pr] if odd else pltpu.roll(v_p[pr], HEAD_DIM, 1)
        sa_src = sa[pr] if odd else pltpu.roll(sa[pr], HEAD_DIM, 1)
        mak_l = packed(mak4[pr][rs])[0]
        x0 = sa_src + _dg(mak_l, rows3(v_src), _NN)
        n_h = n4[pr][rs]
        z.append(jnp.concatenate([n_h[:, :2 * cc], jnp.where(first, n_h[:, 2 * cc:], x0)], axis=1))
    for kk in range(n_doubling):
        for h in heads:
            lhs, rhs = packed(z[h])
            out = _dg(lhs, rhs, _NN)
            z[h] = jnp.where(z_n, out, z[h] + out)
    for pr in pairs:
        x_even = pltpu.roll(z[2 * pr][:, 2 * cc:], HEAD_DIM, 1)
        u_p = jnp.where(first, x_even, z[2 * pr + 1][:, 2 * cc:])
        uv1 = jnp.concatenate([u_p, v_p[pr]], axis=0)
        o_st = _dg(mr[pr], uv1.astype(BF16), _NN)
        bb, ppr = pr // n_pairs, pr % n_pairs
        o_ref[bb, :, PAIR_LANES * ppr:PAIR_LANES * (ppr + 1)] = sr[pr] + jnp.where(first, o_st[:cc], o_st[cc:])
        uvh, uvl = _split(uv1)
        bk = jnp.concatenate([bh_u[pr], kh_u[pr]], axis=0)
        upd = _dg(jnp.concatenate([uvh, uvh, uvl], axis=0), rows3(bk), _TN)
        s_scr[bb, ppr] = jnp.where(bd, pe_u[pr] * s_old[pr] + upd, 0.0)

    @pl.when(c_id == pl.num_programs(1) - 1)
    def _():
        sfin_ref[...] = s_scr[...]


def _chunk_scan(r, w, k, v, a, b, s0_bd, *, nb):
    a_n, t_n, wd = r.shape
    n_pairs = wd // PAIR_LANES
    grid = (a_n // nb, t_n // SCAN_CHUNK)
    tok = pl.BlockSpec((nb, SCAN_CHUNK, wd), lambda i, t: (i, t, 0))
    st = pl.BlockSpec((nb, n_pairs, PAIR_LANES, PAIR_LANES), lambda i, t: (i, 0, 0, 0))
    return pl.pallas_call(
        _chunk_scan_kernel,
        grid=grid,
        in_specs=[tok] * 6 + [st],
        out_specs=[tok, st],
        out_shape=[jax.ShapeDtypeStruct(r.shape, F32), jax.ShapeDtypeStruct(s0_bd.shape, F32)],
        scratch_shapes=[pltpu.VMEM((nb, n_pairs, PAIR_LANES, PAIR_LANES), F32)],
        compiler_params=_cparams(("arbitrary", "arbitrary")),
        name="wkv_chunk_scan",
    )(r, w, k, v, a, b, s0_bd)


def _state_to_blockdiag(s):
    a_n = s.shape[0]
    s = s.reshape(a_n, N_RWKV_HEADS // 2, 2, HEAD_DIM, HEAD_DIM)
    z = jnp.zeros_like(s[:, :, 0])
    top = jnp.concatenate([s[:, :, 0], z], axis=-1)
    bot = jnp.concatenate([z, s[:, :, 1]], axis=-1)
    return jnp.concatenate([top, bot], axis=-2)


def _state_from_blockdiag(s):
    a_n = s.shape[0]
    s = s.reshape(a_n, N_RWKV_HEADS // 2, 2, HEAD_DIM, 2, HEAD_DIM)
    return jnp.stack([s[:, :, 0, :, 0, :], s[:, :, 1, :, 1, :]], axis=2).reshape(
        a_n, N_RWKV_HEADS, HEAD_DIM, HEAD_DIM)


def _attn_kernel(sink_ref, q_ref, kp_ref, vp_ref, kc_ref, vc_ref, o_ref, *, sb, tq, block_prev):
    rows = GROUP * tq
    assert tq & (tq - 1) == 0
    t_row = lax.broadcasted_iota(jnp.int32, (rows, WINDOW), 0) & (tq - 1)
    c_prev = lax.broadcasted_iota(jnp.int32, (rows, WINDOW), 1)
    if block_prev:
        t_row = t_row + jnp.where(pl.program_id(1) >= 1, 0, WINDOW)
    prev_mask = c_prev > t_row
    t_row_c = lax.broadcasted_iota(jnp.int32, (rows, tq), 0) & (tq - 1)
    c_cur = lax.broadcasted_iota(jnp.int32, (rows, tq), 1)
    cur_mask = c_cur <= t_row_c
    grp = lax.broadcasted_iota(jnp.int32, (rows, 1), 0) >> (tq.bit_length() - 1)
    scale = HEAD_DIM ** -0.5
    for s in range(sb):
        q = q_ref[s]
        outs = []
        for h2 in range(N_KV_HEADS):
            hs = slice(HEAD_DIM * h2, HEAD_DIM * (h2 + 1))
            kp = kp_ref[s, :, hs].astype(BF16)
            vp = vp_ref[s, :, hs].astype(BF16)
            kc = kc_ref[s, :, hs].astype(BF16)
            vc = vc_ref[s, :, hs].astype(BF16)
            qh = jnp.concatenate(
                [q[:, HEAD_DIM * (h2 * GROUP + g):HEAD_DIM * (h2 * GROUP + g + 1)] for g in range(GROUP)], axis=0)
            qh = qh.astype(BF16)
            sink = jnp.zeros((rows, 1), F32)
            for g in range(GROUP):
                sink = jnp.where(grp == g, sink_ref[h2 * GROUP + g], sink)
            s_p = jnp.where(prev_mask, _dot_nt(qh, kp) * scale, NEG_INF)
            s_c = jnp.where(cur_mask, _dot_nt(qh, kc) * scale, NEG_INF)
            m = jnp.maximum(jnp.maximum(jnp.max(s_p, axis=-1, keepdims=True),
                                        jnp.max(s_c, axis=-1, keepdims=True)), sink)
            p_p = jnp.exp(s_p - m)
            p_c = jnp.exp(s_c - m)
            den = (jnp.sum(p_p, axis=-1, keepdims=True) + jnp.sum(p_c, axis=-1, keepdims=True)
                   + jnp.exp(sink - m))
            inv = 1.0 / den
            o = _dot((p_p * inv).astype(BF16), vp) + _dot((p_c * inv).astype(BF16), vc)
            outs.extend(o[g * tq:(g + 1) * tq, :] for g in range(GROUP))
        o_ref[s] = jnp.concatenate(outs, axis=-1)


def _attn(sinks, q, k_prev, v_prev, k_cur, v_cur, *, sb, tq, block_prev):
    a_n, t_n, _ = q.shape
    grid = (a_n // sb, t_n // tq)
    cur = lambda c: pl.BlockSpec((sb, tq, c), lambda a, t: (a, t, 0))
    if block_prev:
        prev = pl.BlockSpec((sb, WINDOW, KV_WIDTH), lambda a, t: (a, jnp.maximum(t - 1, 0), 0))
    else:
        prev = pl.BlockSpec((sb, WINDOW, KV_WIDTH), lambda a, t: (a, 0, 0))
    return pl.pallas_call(
        functools.partial(_attn_kernel, sb=sb, tq=tq, block_prev=block_prev),
        grid=grid,
        in_specs=[pl.BlockSpec(memory_space=pltpu.SMEM), cur(ATT_WIDTH), prev, prev, cur(KV_WIDTH), cur(KV_WIDTH)],
        out_specs=cur(ATT_WIDTH),
        out_shape=jax.ShapeDtypeStruct(q.shape, F32),
        compiler_params=_cparams(("arbitrary", "arbitrary")),
        name="swa_attn",
    )(sinks, q, k_prev, v_prev, k_cur, v_cur)


def _mix_out_kernel(x_ref, gt_ref, o_ref, bonus_ref, g_ref, att_ref, lnw_ref, lnb_ref, ones_ref, wo_ref, y_ref):
    x3 = x_ref[...]
    ab, tb, _ = x3.shape
    tm = ab * tb
    ones = ones_ref[...]
    o = o_ref[...].reshape(tm, RWKV_WIDTH)
    mu = _split_sum(o, ones) * (1.0 / HEAD_DIM)
    oc = o - mu
    var = _split_sum(oc * oc, ones) * (1.0 / HEAD_DIM)
    on = oc * lax.rsqrt(var + LNX_EPS) * lnw_ref[...] + lnb_ref[...]
    rw = (on + bonus_ref[...].reshape(tm, RWKV_WIDTH)) * g_ref[...].reshape(tm, RWKV_WIDTH)
    att = att_ref[...].reshape(tm, ATT_WIDTH)
    m = _dot(rw.astype(BF16), wo_ref[:RWKV_WIDTH, :]) + _dot(att.astype(BF16), wo_ref[RWKV_WIDTH:, :])
    y_ref[...] = x3 + gt_ref[...] * m.reshape(ab, tb, D_MODEL)


def _mix_out(x, mods, o, bonus, g, att, p, *, ab, tb):
    a_n, t_n, _ = x.shape
    grid = (a_n // ab, t_n // tb)
    tok = lambda c: pl.BlockSpec((ab, tb, c), lambda a, t: (a, t, 0))
    w = RWKV_WIDTH
    return pl.pallas_call(
        _mix_out_kernel,
        grid=grid,
        in_specs=[
            tok(D_MODEL),
            pl.BlockSpec((ab, 1, D_MODEL), lambda a, t: (a, 0, 5)),
            tok(w), tok(w), tok(w), tok(ATT_WIDTH),
            _const_spec((1, w)), _const_spec((1, w)), _const_spec((w, w)),
            _const_spec((w + ATT_WIDTH, D_MODEL)),
        ],
        out_specs=tok(D_MODEL),
        out_shape=jax.ShapeDtypeStruct(x.shape, F32),
        compiler_params=_cparams(("arbitrary", "arbitrary")),
        name="mix_out",
    )(x, mods, o, bonus, g, att, p["ln_x_w"], p["ln_x_b"], p["head_ones"], p["w_out"])


def _rope_tables(pos, width):
    half = ROPE_DIM // 2
    inv_freq = ROPE_THETA ** (-jnp.arange(0, ROPE_DIM, 2, dtype=F32) / ROPE_DIM)
    ang = pos.astype(F32)[:, None] * inv_freq[None, :]
    cos, sin = jnp.cos(ang), jnp.sin(ang)
    t = pos.shape[0]
    pad = jnp.zeros((t, HEAD_DIM - ROPE_DIM), F32)
    zero = jnp.zeros((t, half), F32)
    c_head = jnp.concatenate([cos, cos, pad + 1.0], axis=-1)
    sa_head = jnp.concatenate([zero, sin, pad], axis=-1)
    sb_head = jnp.concatenate([-sin, zero, pad], axis=-1)
    reps = width // HEAD_DIM
    tile = lambda x: jnp.tile(x, (1, reps))[None]
    return tile(c_head), tile(sa_head), tile(sb_head)


def _state_to_lanes(s):
    a_n = s.shape[0]
    s = s.reshape(a_n, N_RWKV_HEADS // SCAN_HEADS, SCAN_HEADS, HEAD_DIM, HEAD_DIM)
    return s.transpose(0, 1, 3, 2, 4).reshape(a_n, N_RWKV_HEADS // SCAN_HEADS, HEAD_DIM, SCAN_LANES)


def _state_from_lanes(s):
    a_n = s.shape[0]
    s = s.reshape(a_n, N_RWKV_HEADS // SCAN_HEADS, HEAD_DIM, SCAN_HEADS, HEAD_DIM)
    return s.transpose(0, 1, 3, 2, 4).reshape(a_n, N_RWKV_HEADS, HEAD_DIM, HEAD_DIM)


def _layer(x, mods, s0, u_prev0, k_buf, v_buf, pos0, p, *, ab, tb, scan_ab, scan_tb, attn_sb, final_norm_w):
    a_n, t_n, _ = x.shape
    x = _ffn(x, mods, p["norm_ffn1"], p["ffn1_w_gu"], p["ffn1_w_down"], final_norm_w,
             mod_base=0, ab=ab, tb=tb, final_norm=False)
    pos = pos0 + jnp.arange(t_n)
    rope = _rope_tables(pos, ATT_WIDTH) + _rope_tables(pos, KV_WIDTH)
    mix_ab, mix_tb = (ab // 2, tb) if ab > 1 else (ab, tb // 2)
    (r, d, k, v, a, b, g, bonus, q, k_att, v_att, u_last) = _mix_in(x, mods, u_prev0, p, rope,
                                                                   ab=mix_ab, tb=mix_tb)
    if t_n % SCAN_CHUNK == 0:
        o, s_fin = _chunk_scan(r, d, k, v, a, b, _state_to_blockdiag(s0), nb=CHUNK_SCAN_SEQS)
        s_fin = _state_from_blockdiag(s_fin)
    else:
        o, s_fin = _scan(r, d, k, v, a, b, _state_to_lanes(s0), p["scan_ones"], ab=scan_ab, tb=scan_tb)
        s_fin = _state_from_lanes(s_fin)
    if k_buf is None:
        att = _attn(p["attn_sinks"], q, k_att, v_att, k_att, v_att, sb=1, tq=WINDOW, block_prev=True)
        k_new, v_new = k_att[:, -WINDOW:], v_att[:, -WINDOW:]
    else:
        kb = k_buf.reshape(a_n, WINDOW, KV_WIDTH)
        vb = v_buf.reshape(a_n, WINDOW, KV_WIDTH)
        att = _attn(p["attn_sinks"], q, kb, vb, k_att, v_att, sb=attn_sb, tq=t_n, block_prev=False)
        k_new = jnp.concatenate([kb, k_att], axis=1)[:, -WINDOW:]
        v_new = jnp.concatenate([vb, v_att], axis=1)[:, -WINDOW:]
    x = _mix_out(x, mods, o, bonus, g, att, p, ab=ab, tb=tb)
    y = _ffn(x, mods, p["norm_ffn2"], p["ffn2_w_gu"], p["ffn2_w_down"], final_norm_w,
             mod_base=6, ab=ab, tb=tb, final_norm=True)
    kv_shape = (a_n, WINDOW, N_KV_HEADS, HEAD_DIM)
    return y, (s_fin, u_last[:, 0, :], k_new.reshape(kv_shape), v_new.reshape(kv_shape))


def kernel(x_prompt, x_sample, c_prompt, c_sample, state_wkv, state_shift, cache_k, cache_v, w_ada, b_ada,
           norm_ffn1, ffn1_w_gu, ffn1_w_down, norm_mix, w_in, mu_shift, rwkv_w0, rwkv_w2, rwkv_a0, rwkv_a2,
           rwkv_g2, rwkv_k_k, rwkv_k_a, rwkv_r_k, ln_x_w, ln_x_b, attn_sinks, w_out, norm_ffn2, ffn2_w_gu,
           ffn2_w_down, norm_final):
    depth = w_ada.shape[0]
    assert depth == 1, "final norm is fused into the last layer's second FFN"
    bp = x_prompt.shape[0]
    bs = x_sample.shape[0]
    w = RWKV_WIDTH
    lane_head = jnp.arange(w) // HEAD_DIM
    head_ones = (lane_head[:, None] == lane_head[None, :]).astype(BF16)
    fnw = norm_final.reshape(1, 1, D_MODEL)

    n_c = bp + bs
    n_pad = -n_c % 8
    c_all = jnp.concatenate([c_prompt, c_sample, jnp.zeros((n_pad, D_MODEL), F32)], axis=0)

    l = 0
    p = {
        "norm_ffn1": norm_ffn1[l].reshape(1, 1, D_MODEL),
        "ffn1_w_gu": ffn1_w_gu[l].astype(BF16), "ffn1_w_down": ffn1_w_down[l].astype(BF16),
        "norm_mix": norm_mix[l].reshape(1, 1, D_MODEL),
        "w_in": w_in[l].astype(BF16),
        "mu_shift": mu_shift[l].reshape(1, 1, RWKV_COLS),
        "rwkv_w0": rwkv_w0[l].reshape(1, w), "rwkv_w2": rwkv_w2[l].astype(BF16),
        "rwkv_a0": rwkv_a0[l].reshape(1, w), "rwkv_a2": rwkv_a2[l].astype(BF16),
        "rwkv_g2": rwkv_g2[l].astype(BF16),
        "rwkv_k_k": rwkv_k_k[l].reshape(1, w), "rwkv_k_a": rwkv_k_a[l].reshape(1, w),
        "rwkv_r_k": rwkv_r_k[l].reshape(1, w),
        "ln_x_w": ln_x_w[l].reshape(1, w), "ln_x_b": ln_x_b[l].reshape(1, w),
        "attn_sinks": attn_sinks[l],
        "w_out": w_out[l].astype(BF16),
        "norm_ffn2": norm_ffn2[l].reshape(1, 1, D_MODEL),
        "ffn2_w_gu": ffn2_w_gu[l].astype(BF16), "ffn2_w_down": ffn2_w_down[l].astype(BF16),
        "head_ones": head_ones,
        "scan_ones": head_ones[:SCAN_LANES, :SCAN_LANES],
    }
    mods_all = _ada(c_all, w_ada[l], b_ada[l])
    mods_p = mods_all[:bp].reshape(bp, 1, N_ADA * D_MODEL)
    mods_s = mods_all[bp:n_c].reshape(bs, 1, N_ADA * D_MODEL)

    s0_p = jnp.zeros((bp, N_RWKV_HEADS, HEAD_DIM, HEAD_DIM), F32)
    u0_p = jnp.zeros((bp, 1, RWKV_COLS), F32)
    yp, st_p = _layer(x_prompt, mods_p, s0_p, u0_p, None, None, 0, p,
                      ab=1, tb=512, scan_ab=bp, scan_tb=128, attn_sb=1, final_norm_w=fnw)
    t_s = x_sample.shape[1]
    ys, st_s = _layer(x_sample, mods_s, state_wkv[l], state_shift[l].reshape(bs, 1, RWKV_COLS),
                      cache_k[l], cache_v[l], PAST_LEN, p,
                      ab=512 // t_s, tb=t_s, scan_ab=4, scan_tb=t_s, attn_sb=8, final_norm_w=fnw)
    return (yp, ys, st_p[0][None], st_p[1][None], st_p[2][None], st_p[3][None],
            st_s[0][None], st_s[1][None], st_s[2][None], st_s[3][None])
```

```python
import functools

import jax
import jax.numpy as jnp
from jax import lax
from jax.experimental import pallas as pl
from jax.experimental.pallas import tpu as pltpu

F32 = jnp.float32
BF16 = jnp.bfloat16

D_MODEL = 1024
HEAD_DIM = 64
N_RWKV_HEADS = 8
RWKV_WIDTH = N_RWKV_HEADS * HEAD_DIM
N_Q_HEADS = 8
N_KV_HEADS = 2
GROUP = N_Q_HEADS // N_KV_HEADS
ATT_WIDTH = N_Q_HEADS * HEAD_DIM
KV_WIDTH = N_KV_HEADS * HEAD_DIM
DECAY_LORA = 64
AAA_LORA = 64
GATE_LORA = 128
RWKV_COLS = 3 * RWKV_WIDTH + DECAY_LORA + AAA_LORA + GATE_LORA
IN_COLS = RWKV_COLS + ATT_WIDTH + 2 * KV_WIDTH
WINDOW = 128
ROPE_THETA = 500000.0
ROPE_DIM = HEAD_DIM // 4
D_FF = 2816
N_ADA = 9
NORM_EPS = 1e-5
LNX_EPS = 64e-5
NEG_INF = -1e30
PAST_LEN = 16384

V7X_VMEM_LIMIT_BYTES = 56 * 1024 * 1024
FFN_CHUNK = 256
SCAN_LANES = 256
SCAN_HEADS = SCAN_LANES // HEAD_DIM
SCAN_CHUNK = 64
PAIR_LANES = 2 * HEAD_DIM
CHUNK_SCAN_SEQS = 4


def _cparams(sem):
    return pltpu.CompilerParams(dimension_semantics=sem, vmem_limit_bytes=V7X_VMEM_LIMIT_BYTES)


def _const_spec(shape):
    nd = len(shape)
    return pl.BlockSpec(shape, lambda *_: (0,) * nd)


def _dot(a, b):
    return jnp.dot(a, b, preferred_element_type=F32)


def _dot_nt(a, b):
    return lax.dot_general(a, b, (((1,), (1,)), ((), ())), preferred_element_type=F32)


def _split_sum(x, ones):
    hi = x.astype(BF16)
    lo = (x - hi.astype(F32)).astype(BF16)
    return _dot(hi, ones) + _dot(lo, ones)


def _mod_norm(x3, nw, sc, sh):
    ms = jnp.mean(x3 * x3, axis=-1, keepdims=True)
    y = x3 * lax.rsqrt(ms + NORM_EPS) * nw
    return y * (1.0 + sc) + sh


def _ada_kernel(c_ref, w_ref, b_ref, o_ref):
    c = c_ref[...]
    s = (c * jax.nn.sigmoid(c)).astype(BF16)
    o_ref[...] = _dot(s, w_ref[...].astype(BF16)) + b_ref[...]


def _ada(c_all, w_ada, b_ada):
    n = c_all.shape[0]
    cols = N_ADA * D_MODEL
    return pl.pallas_call(
        _ada_kernel,
        grid=(N_ADA,),
        in_specs=[
            pl.BlockSpec((n, D_MODEL), lambda i: (0, 0)),
            pl.BlockSpec((D_MODEL, D_MODEL), lambda i: (0, i)),
            pl.BlockSpec((1, D_MODEL), lambda i: (0, i)),
        ],
        out_specs=pl.BlockSpec((n, D_MODEL), lambda i: (0, i)),
        out_shape=jax.ShapeDtypeStruct((n, cols), F32),
        compiler_params=_cparams(("arbitrary",)),
        name="ada",
    )(c_all, w_ada, b_ada.reshape(1, cols))


def _ffn_kernel(x_ref, sh_ref, sc_ref, gt_ref, nw_ref, wgu_ref, wd_ref, fn_ref, o_ref, h_scr, acc_scr,
                *, final_norm):
    x3 = x_ref[...]
    ab, tb, _ = x3.shape
    h3 = _mod_norm(x3, nw_ref[...], sc_ref[...], sh_ref[...])
    h_scr[...] = h3.reshape(ab * tb, D_MODEL).astype(BF16)
    for c in range(D_FF // FFN_CHUNK):
        lo = c * FFN_CHUNK
        h = h_scr[...]
        g = _dot(h, wgu_ref[:, lo:lo + FFN_CHUNK])
        u = _dot(h, wgu_ref[:, D_FF + lo:D_FF + lo + FFN_CHUNK])
        act = (g * jax.nn.sigmoid(g) * u).astype(BF16)
        part = _dot(act, wd_ref[lo:lo + FFN_CHUNK, :])
        if c == 0:
            acc_scr[...] = part
        else:
            acc_scr[...] += part
    y3 = x3 + (0.5 * gt_ref[...]) * acc_scr[...].reshape(ab, tb, D_MODEL)
    if final_norm:
        ms = jnp.mean(y3 * y3, axis=-1, keepdims=True)
        y3 = y3 * lax.rsqrt(ms + NORM_EPS) * fn_ref[...]
    o_ref[...] = y3


def _ffn(x, mods, nw, wgu, wd, fnw, *, mod_base, ab, tb, final_norm):
    a_n, t_n, _ = x.shape
    grid = (a_n // ab, t_n // tb)
    mod_spec = lambda k: pl.BlockSpec((ab, 1, D_MODEL), lambda a, t, k=k: (a, 0, mod_base + k))
    return pl.pallas_call(
        functools.partial(_ffn_kernel, final_norm=final_norm),
        grid=grid,
        in_specs=[
            pl.BlockSpec((ab, tb, D_MODEL), lambda a, t: (a, t, 0)),
            mod_spec(0), mod_spec(1), mod_spec(2),
            _const_spec((1, 1, D_MODEL)),
            pl.BlockSpec((D_MODEL, 2 * D_FF), lambda a, t: (0, 0), pipeline_mode=pl.Buffered(1)),
            pl.BlockSpec((D_FF, D_MODEL), lambda a, t: (0, 0), pipeline_mode=pl.Buffered(1)),
            _const_spec((1, 1, D_MODEL)),
        ],
        out_specs=pl.BlockSpec((ab, tb, D_MODEL), lambda a, t: (a, t, 0)),
        out_shape=jax.ShapeDtypeStruct(x.shape, F32),
        scratch_shapes=[pltpu.VMEM((ab * tb, D_MODEL), BF16), pltpu.VMEM((ab * tb, D_MODEL), F32)],
        compiler_params=_cparams(("arbitrary", "arbitrary")),
        name="ffn_final" if final_norm else "ffn",
    )(x, mods, mods, mods, nw, wgu, wd, fnw)


def _mix_in_kernel(x_ref, sh_ref, sc_ref, nw_ref, win_ref, uprev_ref, mu_ref, w0_ref, w2_ref, a0_ref, a2_ref,
                   g2_ref, kk_ref, ka_ref, rk_ref, ones_ref, cq_ref, sqa_ref, sqb_ref, ck_ref, ska_ref, skb_ref,
                   r_out, d_out, k_out, v_out, a_out, b_out, g_out, bonus_out, q_out, ka_out, va_out, ulast_out,
                   carry_scr):
    t_id = pl.program_id(1)
    x3 = x_ref[...]
    ab, tb, _ = x3.shape
    tm = ab * tb
    h = _mod_norm(x3, nw_ref[...], sc_ref[...], sh_ref[...]).reshape(tm, D_MODEL).astype(BF16)
    proj = _dot(h, win_ref[...])

    u3 = proj[:, :RWKV_COLS].reshape(ab, tb, RWKV_COLS)

    @pl.when(t_id == 0)
    def _():
        carry_scr[...] = uprev_ref[...]

    prev = carry_scr[...]
    t_idx = lax.broadcasted_iota(jnp.int32, (ab, tb, RWKV_COLS), 1)
    u_shift = jnp.where(t_idx == 0, prev, pltpu.roll(u3, 1, 1))
    last = u3[:, tb - 1:tb, :]
    carry_scr[...] = last
    ulast_out[...] = last

    z = (u3 + (u_shift - u3) * mu_ref[...]).reshape(tm, RWKV_COLS)
    w = RWKV_WIDTH
    r = z[:, :w]
    k = z[:, w:2 * w]
    v = z[:, 2 * w:3 * w]
    zw = z[:, 3 * w:3 * w + DECAY_LORA]
    za = z[:, 3 * w + DECAY_LORA:3 * w + DECAY_LORA + AAA_LORA]
    zg = z[:, 3 * w + DECAY_LORA + AAA_LORA:]

    y = -(w0_ref[...] + _dot(jnp.tanh(zw).astype(BF16), w2_ref[...]))
    softplus = jnp.maximum(y, 0.0) + jnp.log1p(jnp.exp(-jnp.abs(y)))
    w_log = -softplus - 0.5
    log_decay = -jnp.exp(w_log)
    a = jax.nn.sigmoid(a0_ref[...] + _dot(za.astype(BF16), a2_ref[...]))
    g = _dot(jax.nn.sigmoid(zg).astype(BF16), g2_ref[...])

    ones = ones_ref[...]
    kk = k * kk_ref[...]
    kk = kk * lax.rsqrt(jnp.maximum(_split_sum(kk * kk, ones), 1e-24))
    k_mod = k * (1.0 + (a - 1.0) * ka_ref[...])
    bonus = _split_sum(r * k_mod * rk_ref[...], ones) * v

    shp = (ab, tb, w)
    r_out[...] = r.reshape(shp)
    d_out[...] = log_decay.reshape(shp)
    k_out[...] = k_mod.reshape(shp)
    v_out[...] = v.reshape(shp)
    a_out[...] = (-kk).reshape(shp)
    b_out[...] = (kk * a).reshape(shp)
    g_out[...] = g.reshape(shp)
    bonus_out[...] = bonus.reshape(shp)

    q3 = proj[:, RWKV_COLS:RWKV_COLS + ATT_WIDTH].reshape(ab, tb, ATT_WIDTH)
    q_out[...] = (q3 * cq_ref[...] + pltpu.roll(q3, ROPE_DIM // 2, 2) * sqa_ref[...]
                  + pltpu.roll(q3, ATT_WIDTH - ROPE_DIM // 2, 2) * sqb_ref[...])
    k3 = proj[:, RWKV_COLS + ATT_WIDTH:RWKV_COLS + ATT_WIDTH + KV_WIDTH].reshape(ab, tb, KV_WIDTH)
    ka_out[...] = (k3 * ck_ref[...] + pltpu.roll(k3, ROPE_DIM // 2, 2) * ska_ref[...]
                   + pltpu.roll(k3, KV_WIDTH - ROPE_DIM // 2, 2) * skb_ref[...])
    va_out[...] = proj[:, RWKV_COLS + ATT_WIDTH + KV_WIDTH:].reshape(ab, tb, KV_WIDTH)


def _mix_in(x, mods, u_prev0, p, rope, *, ab, tb):
    a_n, t_n, _ = x.shape
    grid = (a_n // ab, t_n // tb)
    tok = lambda c: pl.BlockSpec((ab, tb, c), lambda a, t: (a, t, 0))
    seq = lambda c: pl.BlockSpec((ab, 1, c), lambda a, t: (a, 0, 0))
    mod_spec = lambda k: pl.BlockSpec((ab, 1, D_MODEL), lambda a, t, k=k: (a, 0, k))
    rope_spec = lambda c: pl.BlockSpec((1, tb, c), lambda a, t: (0, t, 0))
    w = RWKV_WIDTH
    tok_shape = lambda c: jax.ShapeDtypeStruct((a_n, t_n, c), F32)
    return pl.pallas_call(
        _mix_in_kernel,
        grid=grid,
        in_specs=[
            tok(D_MODEL), mod_spec(3), mod_spec(4),
            _const_spec((1, 1, D_MODEL)),
            pl.BlockSpec((D_MODEL, IN_COLS), lambda a, t: (0, 0), pipeline_mode=pl.Buffered(1)),
            seq(RWKV_COLS),
            _const_spec((1, 1, RWKV_COLS)),
            _const_spec((1, w)), _const_spec((DECAY_LORA, w)),
            _const_spec((1, w)), _const_spec((AAA_LORA, w)),
            _const_spec((GATE_LORA, w)),
            _const_spec((1, w)), _const_spec((1, w)), _const_spec((1, w)),
            _const_spec((w, w)),
            rope_spec(ATT_WIDTH), rope_spec(ATT_WIDTH), rope_spec(ATT_WIDTH),
            rope_spec(KV_WIDTH), rope_spec(KV_WIDTH), rope_spec(KV_WIDTH),
        ],
        out_specs=[tok(w)] * 8 + [tok(ATT_WIDTH), tok(KV_WIDTH), tok(KV_WIDTH), seq(RWKV_COLS)],
        out_shape=[tok_shape(w)] * 8 + [tok_shape(ATT_WIDTH), tok_shape(KV_WIDTH), tok_shape(KV_WIDTH),
                                        jax.ShapeDtypeStruct((a_n, 1, RWKV_COLS), F32)],
        scratch_shapes=[pltpu.VMEM((ab, 1, RWKV_COLS), F32)],
        compiler_params=_cparams(("arbitrary", "arbitrary")),
        name="mix_in",
    )(x, mods, mods, p["norm_mix"], p["w_in"], u_prev0, p["mu_shift"], p["rwkv_w0"], p["rwkv_w2"],
      p["rwkv_a0"], p["rwkv_a2"], p["rwkv_g2"], p["rwkv_k_k"], p["rwkv_k_a"], p["rwkv_r_k"], p["head_ones"],
      *rope)


def _scan_kernel(r_ref, d_ref, k_ref, v_ref, a_ref, b_ref, s0_ref, ones_ref, o_ref, sfin_ref, s_scr, *, ab, tb):
    t_id = pl.program_id(1)

    @pl.when(t_id == 0)
    def _():
        s_scr[...] = s0_ref[...]

    ones = ones_ref[...]
    row_i = lax.broadcasted_iota(jnp.int32, (HEAD_DIM, SCAN_LANES), 0)
    lane_i = lax.broadcasted_iota(jnp.int32, (HEAD_DIM, SCAN_LANES), 1)
    diag = row_i == (lane_i & (HEAD_DIM - 1))
    n_groups = RWKV_WIDTH // SCAN_LANES

    def tile_body(tt, carry):
        base = pl.multiple_of(tt * 8, 8)
        for b in range(ab):
            for hg in range(n_groups):
                ls = slice(SCAN_LANES * hg, SCAN_LANES * (hg + 1))
                rr = r_ref[b, pl.ds(base, 8), ls]
                dd = jnp.exp(d_ref[b, pl.ds(base, 8), ls])
                kk = k_ref[b, pl.ds(base, 8), ls]
                vv = v_ref[b, pl.ds(base, 8), ls]
                aa = a_ref[b, pl.ds(base, 8), ls]
                bb = b_ref[b, pl.ds(base, 8), ls]
                s = s_scr[b, hg]
                rows = []
                for i in range(8):
                    v_col = _split_sum(jnp.where(diag, vv[i:i + 1, :], 0.0), ones)
                    sa = _split_sum(s * aa[i:i + 1, :], ones)
                    s = s * dd[i:i + 1, :] + sa * bb[i:i + 1, :] + v_col * kk[i:i + 1, :]
                    o_b = _split_sum(s * rr[i:i + 1, :], ones)
                    rows.append(jnp.sum(jnp.where(diag, o_b, 0.0), axis=0, keepdims=True))
                s_scr[b, hg] = s
                o_ref[b, pl.ds(base, 8), ls] = jnp.concatenate(rows, axis=0)
        return carry

    lax.fori_loop(0, tb // 8, tile_body, 0)

    @pl.when(t_id == pl.num_programs(1) - 1)
    def _():
        sfin_ref[...] = s_scr[...]


def _scan(r, d, k, v, a, b, s0, ones, *, ab, tb):
    a_n, t_n, w = r.shape
    n_groups = w // SCAN_LANES
    grid = (a_n // ab, t_n // tb)
    tok = pl.BlockSpec((ab, tb, w), lambda i, t: (i, t, 0))
    st = pl.BlockSpec((ab, n_groups, HEAD_DIM, SCAN_LANES), lambda i, t: (i, 0, 0, 0))
    return pl.pallas_call(
        functools.partial(_scan_kernel, ab=ab, tb=tb),
        grid=grid,
        in_specs=[tok] * 6 + [st, _const_spec((SCAN_LANES, SCAN_LANES))],
        out_specs=[tok, st],
        out_shape=[jax.ShapeDtypeStruct(r.shape, F32), jax.ShapeDtypeStruct(s0.shape, F32)],
        scratch_shapes=[pltpu.VMEM((ab, n_groups, HEAD_DIM, SCAN_LANES), F32)],
        compiler_params=_cparams(("arbitrary", "arbitrary")),
        name="wkv_scan",
    )(r, d, k, v, a, b, s0, ones)


_NN = (((1,), (0,)), ((), ()))
_NT = (((1,), (1,)), ((), ()))
_TN = (((0,), (0,)), ((), ()))


def _dg(a, b, dn):
    return lax.dot_general(a, b, dn, preferred_element_type=F32)


def _split(x):
    hi = x.astype(BF16)
    return hi, (x - hi.astype(F32)).astype(BF16)


def _dot3(a, b, dn=_NN):
    ah, al = _split(a)
    bh, bl = _split(b)
    return _dg(ah, bh, dn) + _dg(ah, bl, dn) + _dg(al, bh, dn)


def _dot1(a, b, dn=_NN):
    return _dg(a.astype(BF16), b.astype(BF16), dn)


def _chunk_scan_kernel(r_ref, w_ref, k_ref, v_ref, a_ref, b_ref, s0_ref, o_ref, sfin_ref, s_scr):
    c_id = pl.program_id(1)

    @pl.when(c_id == 0)
    def _():
        s_scr[...] = s0_ref[...]

    cc = SCAN_CHUNK
    nb = r_ref.shape[0]
    n_pairs = RWKV_WIDTH // PAIR_LANES
    row = lax.broadcasted_iota(jnp.int32, (cc, cc), 0)
    col = lax.broadcasted_iota(jnp.int32, (cc, cc), 1)
    ltri = jnp.where(row >= col, 1.0, 0.0).astype(BF16)

    am_u, bp_u, kp_u, ro_u, bh_u, kh_u, pe_u, v_u = [], [], [], [], [], [], [], []
    for bb in range(nb):
        w = w_ref[bb]
        wh, wl = _split(w)
        c_in = _dg(ltri, wh, _NN) + _dg(ltri, wl, _NN)
        c_end = c_in[cc - 1:cc, :]
        e_in = jnp.exp(c_in)
        e_ex = jnp.exp(c_in - w)
        e_neg = jnp.exp(-c_in)
        e_tail = jnp.exp(c_end - c_in)
        p_end = jnp.exp(c_end)
        a_all = a_ref[bb]
        b_all = b_ref[bb]
        k_all = k_ref[bb]
        am_all = a_all * e_ex
        bp_all = b_all * e_neg
        kp_all = k_all * e_neg
        ro_all = r_ref[bb] * e_in
        bh_all = b_all * e_tail
        kh_all = k_all * e_tail
        for pr in range(n_pairs):
            ls = slice(PAIR_LANES * pr, PAIR_LANES * (pr + 1))
            am_u.append(am_all[:, ls])
            bp_u.append(bp_all[:, ls])
            kp_u.append(kp_all[:, ls])
            ro_u.append(ro_all[:, ls])
            bh_u.append(bh_all[:, ls])
            kh_u.append(kh_all[:, ls])
            pe_u.append(p_end[:, ls])
            v_u.append(v_ref[bb, :, ls])

    first = lax.broadcasted_iota(jnp.int32, (cc, PAIR_LANES), 1) < HEAD_DIM
    bd = ((lax.broadcasted_iota(jnp.int32, (PAIR_LANES, PAIR_LANES), 0) >= HEAD_DIM)
          == (lax.broadcasted_iota(jnp.int32, (PAIR_LANES, PAIR_LANES), 1) >= HEAD_DIM))
    n_doubling = cc.bit_length() - 1
    wide = 4 * cc
    row_w = lax.broadcasted_iota(jnp.int32, (2 * cc, wide), 0) & (cc - 1)
    col_w = lax.broadcasted_iota(jnp.int32, (2 * cc, wide), 1)
    tri_s_w = row_w > (col_w & (cc - 1))
    col_z = lax.broadcasted_iota(jnp.int32, (cc, wide), 1)
    z_lo = col_z >= 2 * cc
    z_n = col_z < 3 * cc
    row2 = lax.broadcasted_iota(jnp.int32, (2 * cc, 2 * cc), 0) & (cc - 1)
    col2 = lax.broadcasted_iota(jnp.int32, (2 * cc, 2 * cc), 1) & (cc - 1)
    tri_i2 = row2 >= col2

    def rows3(x):
        hi, lo = _split(x)
        return jnp.concatenate([hi, lo, hi], axis=0)

    def lanes3_lhs(x):
        hi, lo = _split(x)
        return jnp.concatenate([hi, hi, lo], axis=1)

    def lanes3_rhs(x):
        hi, lo = _split(x)
        return jnp.concatenate([hi, lo, hi], axis=1)

    def packed(z):
        hi, lo = _split(z)
        return jnp.where(z_lo, lo, hi)[:, :3 * cc], jnp.concatenate([hi, lo, hi], axis=0)

    pairs = range(nb * n_pairs)
    heads = range(2 * nb * n_pairs)
    s_old = [s_scr[pr // n_pairs, pr % n_pairs] for pr in pairs]
    s_t = [s_old[pr].T for pr in pairs]
    v_p = v_u
    sa = [_dg(lanes3_lhs(am_u[pr]), rows3(s_t[pr]), _NN) for pr in pairs]
    sr = [_dot1(ro_u[pr], s_t[pr]) for pr in pairs]

    n4, mak4, mr = [], [], []
    for pr in pairs:
        am_p = am_u[pr]
        ro_p = ro_u[pr]
        am_st = lanes3_lhs(jnp.concatenate([jnp.where(first, am_p, 0.0), jnp.where(first, 0.0, am_p)], axis=0))
        ro_st = jnp.concatenate([jnp.where(first, ro_p, 0.0), jnp.where(first, 0.0, ro_p)], axis=0).astype(BF16)
        b3 = lanes3_rhs(bp_u[pr])
        k3 = lanes3_rhs(kp_u[pr])
        bk1 = jnp.concatenate([bp_u[pr], kp_u[pr]], axis=0).astype(BF16)
        n4.append(jnp.where(tri_s_w, _dg(am_st, jnp.concatenate([b3] * 4, axis=0), _NT), 0.0))
        mak4.append(jnp.where(tri_s_w, _dg(am_st, jnp.concatenate([k3] * 4, axis=0), _NT), 0.0))
        mr.append(jnp.where(tri_i2, _dg(ro_st, bk1, _NT), 0.0).astype(BF16))

    z = []
    for h in heads:
        pr, odd = h // 2, h % 2
        rs = slice(cc * odd, cc * (odd + 1))
        v_src = v_p[pr] if odd else pltpu.roll(v_p[pr], HEAD_DIM, 1)
        sa_src = sa[pr] if odd else pltpu.roll(sa[pr], HEAD_DIM, 1)
        mak_l = packed(mak4[pr][rs])[0]
        x0 = sa_src + _dg(mak_l, rows3(v_src), _NN)
        n_h = n4[pr][rs]
        z.append(jnp.concatenate([n_h[:, :2 * cc], jnp.where(first, n_h[:, 2 * cc:], x0)], axis=1))
    for kk in range(n_doubling):
        for h in heads:
            lhs, rhs = packed(z[h])
            out = _dg(lhs, rhs, _NN)
            z[h] = jnp.where(z_n, out, z[h] + out)
    for pr in pairs:
        x_even = pltpu.roll(z[2 * pr][:, 2 * cc:], HEAD_DIM, 1)
        u_p = jnp.where(first, x_even, z[2 * pr + 1][:, 2 * cc:])
        uv1 = jnp.concatenate([u_p, v_p[pr]], axis=0)
        o_st = _dg(mr[pr], uv1.astype(BF16), _NN)
        bb, ppr = pr // n_pairs, pr % n_pairs
        o_ref[bb, :, PAIR_LANES * ppr:PAIR_LANES * (ppr + 1)] = sr[pr] + jnp.where(first, o_st[:cc], o_st[cc:])
        uvh, uvl = _split(uv1)
        bk = jnp.concatenate([bh_u[pr], kh_u[pr]], axis=0)
        upd = _dg(jnp.concatenate([uvh, uvh, uvl], axis=0), rows3(bk), _TN)
        s_scr[bb, ppr] = jnp.where(bd, pe_u[pr] * s_old[pr] + upd, 0.0)

    @pl.when(c_id == pl.num_programs(1) - 1)
    def _():
        sfin_ref[...] = s_scr[...]


def _chunk_scan(r, w, k, v, a, b, s0_bd, *, nb):
    a_n, t_n, wd = r.shape
    n_pairs = wd // PAIR_LANES
    grid = (a_n // nb, t_n // SCAN_CHUNK)
    tok = pl.BlockSpec((nb, SCAN_CHUNK, wd), lambda i, t: (i, t, 0))
    st = pl.BlockSpec((nb, n_pairs, PAIR_LANES, PAIR_LANES), lambda i, t: (i, 0, 0, 0))
    return pl.pallas_call(
        _chunk_scan_kernel,
        grid=grid,
        in_specs=[tok] * 6 + [st],
        out_specs=[tok, st],
        out_shape=[jax.ShapeDtypeStruct(r.shape, F32), jax.ShapeDtypeStruct(s0_bd.shape, F32)],
        scratch_shapes=[pltpu.VMEM((nb, n_pairs, PAIR_LANES, PAIR_LANES), F32)],
        compiler_params=_cparams(("arbitrary", "arbitrary")),
        name="wkv_chunk_scan",
    )(r, w, k, v, a, b, s0_bd)


def _state_to_blockdiag(s):
    a_n = s.shape[0]
    s = s.reshape(a_n, N_RWKV_HEADS // 2, 2, HEAD_DIM, HEAD_DIM)
    z = jnp.zeros_like(s[:, :, 0])
    top = jnp.concatenate([s[:, :, 0], z], axis=-1)
    bot = jnp.concatenate([z, s[:, :, 1]], axis=-1)
    return jnp.concatenate([top, bot], axis=-2)


def _state_from_blockdiag(s):
    a_n = s.shape[0]
    s = s.reshape(a_n, N_RWKV_HEADS // 2, 2, HEAD_DIM, 2, HEAD_DIM)
    return jnp.stack([s[:, :, 0, :, 0, :], s[:, :, 1, :, 1, :]], axis=2).reshape(
        a_n, N_RWKV_HEADS, HEAD_DIM, HEAD_DIM)


def _attn_kernel(sink_ref, q_ref, kp_ref, vp_ref, kc_ref, vc_ref, o_ref, *, sb, tq, block_prev):
    rows = GROUP * tq
    assert tq & (tq - 1) == 0
    t_row = lax.broadcasted_iota(jnp.int32, (rows, WINDOW), 0) & (tq - 1)
    c_prev = lax.broadcasted_iota(jnp.int32, (rows, WINDOW), 1)
    if block_prev:
        t_row = t_row + jnp.where(pl.program_id(1) >= 1, 0, WINDOW)
    prev_mask = c_prev > t_row
    t_row_c = lax.broadcasted_iota(jnp.int32, (rows, tq), 0) & (tq - 1)
    c_cur = lax.broadcasted_iota(jnp.int32, (rows, tq), 1)
    cur_mask = c_cur <= t_row_c
    grp = lax.broadcasted_iota(jnp.int32, (rows, 1), 0) >> (tq.bit_length() - 1)
    scale = HEAD_DIM ** -0.5
    for s in range(sb):
        q = q_ref[s]
        outs = []
        for h2 in range(N_KV_HEADS):
            hs = slice(HEAD_DIM * h2, HEAD_DIM * (h2 + 1))
            kp = kp_ref[s, :, hs].astype(BF16)
            vp = vp_ref[s, :, hs].astype(BF16)
            kc = kc_ref[s, :, hs].astype(BF16)
            vc = vc_ref[s, :, hs].astype(BF16)
            qh = jnp.concatenate(
                [q[:, HEAD_DIM * (h2 * GROUP + g):HEAD_DIM * (h2 * GROUP + g + 1)] for g in range(GROUP)], axis=0)
            qh = qh.astype(BF16)
            sink = jnp.zeros((rows, 1), F32)
            for g in range(GROUP):
                sink = jnp.where(grp == g, sink_ref[h2 * GROUP + g], sink)
            s_p = jnp.where(prev_mask, _dot_nt(qh, kp) * scale, NEG_INF)
            s_c = jnp.where(cur_mask, _dot_nt(qh, kc) * scale, NEG_INF)
            m = jnp.maximum(jnp.maximum(jnp.max(s_p, axis=-1, keepdims=True),
                                        jnp.max(s_c, axis=-1, keepdims=True)), sink)
            p_p = jnp.exp(s_p - m)
            p_c = jnp.exp(s_c - m)
            den = (jnp.sum(p_p, axis=-1, keepdims=True) + jnp.sum(p_c, axis=-1, keepdims=True)
                   + jnp.exp(sink - m))
            inv = 1.0 / den
            o = _dot((p_p * inv).astype(BF16), vp) + _dot((p_c * inv).astype(BF16), vc)
            outs.extend(o[g * tq:(g + 1) * tq, :] for g in range(GROUP))
        o_ref[s] = jnp.concatenate(outs, axis=-1)


def _attn(sinks, q, k_prev, v_prev, k_cur, v_cur, *, sb, tq, block_prev):
    a_n, t_n, _ = q.shape
    grid = (a_n // sb, t_n // tq)
    cur = lambda c: pl.BlockSpec((sb, tq, c), lambda a, t: (a, t, 0))
    if block_prev:
        prev = pl.BlockSpec((sb, WINDOW, KV_WIDTH), lambda a, t: (a, jnp.maximum(t - 1, 0), 0))
    else:
        prev = pl.BlockSpec((sb, WINDOW, KV_WIDTH), lambda a, t: (a, 0, 0))
    return pl.pallas_call(
        functools.partial(_attn_kernel, sb=sb, tq=tq, block_prev=block_prev),
        grid=grid,
        in_specs=[pl.BlockSpec(memory_space=pltpu.SMEM), cur(ATT_WIDTH), prev, prev, cur(KV_WIDTH), cur(KV_WIDTH)],
        out_specs=cur(ATT_WIDTH),
        out_shape=jax.ShapeDtypeStruct(q.shape, F32),
        compiler_params=_cparams(("arbitrary", "arbitrary")),
        name="swa_attn",
    )(sinks, q, k_prev, v_prev, k_cur, v_cur)


def _mix_out_kernel(x_ref, gt_ref, o_ref, bonus_ref, g_ref, att_ref, lnw_ref, lnb_ref, ones_ref, wo_ref, y_ref):
    x3 = x_ref[...]
    ab, tb, _ = x3.shape
    tm = ab * tb
    ones = ones_ref[...]
    o = o_ref[...].reshape(tm, RWKV_WIDTH)
    mu = _split_sum(o, ones) * (1.0 / HEAD_DIM)
    oc = o - mu
    var = _split_sum(oc * oc, ones) * (1.0 / HEAD_DIM)
    on = oc * lax.rsqrt(var + LNX_EPS) * lnw_ref[...] + lnb_ref[...]
    rw = (on + bonus_ref[...].reshape(tm, RWKV_WIDTH)) * g_ref[...].reshape(tm, RWKV_WIDTH)
    att = att_ref[...].reshape(tm, ATT_WIDTH)
    m = _dot(rw.astype(BF16), wo_ref[:RWKV_WIDTH, :]) + _dot(att.astype(BF16), wo_ref[RWKV_WIDTH:, :])
    y_ref[...] = x3 + gt_ref[...] * m.reshape(ab, tb, D_MODEL)


def _mix_out(x, mods, o, bonus, g, att, p, *, ab, tb):
    a_n, t_n, _ = x.shape
    grid = (a_n // ab, t_n // tb)
    tok = lambda c: pl.BlockSpec((ab, tb, c), lambda a, t: (a, t, 0))
    w = RWKV_WIDTH
    return pl.pallas_call(
        _mix_out_kernel,
        grid=grid,
        in_specs=[
            tok(D_MODEL),
            pl.BlockSpec((ab, 1, D_MODEL), lambda a, t: (a, 0, 5)),
            tok(w), tok(w), tok(w), tok(ATT_WIDTH),
            _const_spec((1, w)), _const_spec((1, w)), _const_spec((w, w)),
            _const_spec((w + ATT_WIDTH, D_MODEL)),
        ],
        out_specs=tok(D_MODEL),
        out_shape=jax.ShapeDtypeStruct(x.shape, F32),
        compiler_params=_cparams(("arbitrary", "arbitrary")),
        name="mix_out",
    )(x, mods, o, bonus, g, att, p["ln_x_w"], p["ln_x_b"], p["head_ones"], p["w_out"])


def _rope_tables(pos, width):
    half = ROPE_DIM // 2
    inv_freq = ROPE_THETA ** (-jnp.arange(0, ROPE_DIM, 2, dtype=F32) / ROPE_DIM)
    ang = pos.astype(F32)[:, None] * inv_freq[None, :]
    cos, sin = jnp.cos(ang), jnp.sin(ang)
    t = pos.shape[0]
    pad = jnp.zeros((t, HEAD_DIM - ROPE_DIM), F32)
    zero = jnp.zeros((t, half), F32)
    c_head = jnp.concatenate([cos, cos, pad + 1.0], axis=-1)
    sa_head = jnp.concatenate([zero, sin, pad], axis=-1)
    sb_head = jnp.concatenate([-sin, zero, pad], axis=-1)
    reps = width // HEAD_DIM
    tile = lambda x: jnp.tile(x, (1, reps))[None]
    return tile(c_head), tile(sa_head), tile(sb_head)


def _state_to_lanes(s):
    a_n = s.shape[0]
    s = s.reshape(a_n, N_RWKV_HEADS // SCAN_HEADS, SCAN_HEADS, HEAD_DIM, HEAD_DIM)
    return s.transpose(0, 1, 3, 2, 4).reshape(a_n, N_RWKV_HEADS // SCAN_HEADS, HEAD_DIM, SCAN_LANES)


def _state_from_lanes(s):
    a_n = s.shape[0]
    s = s.reshape(a_n, N_RWKV_HEADS // SCAN_HEADS, HEAD_DIM, SCAN_HEADS, HEAD_DIM)
    return s.transpose(0, 1, 3, 2, 4).reshape(a_n, N_RWKV_HEADS, HEAD_DIM, HEAD_DIM)


def _layer(x, mods, s0, u_prev0, k_buf, v_buf, pos0, p, *, ab, tb, scan_ab, scan_tb, attn_sb, final_norm_w):
    a_n, t_n, _ = x.shape
    x = _ffn(x, mods, p["norm_ffn1"], p["ffn1_w_gu"], p["ffn1_w_down"], final_norm_w,
             mod_base=0, ab=ab, tb=tb, final_norm=False)
    pos = pos0 + jnp.arange(t_n)
    rope = _rope_tables(pos, ATT_WIDTH) + _rope_tables(pos, KV_WIDTH)
    mix_ab, mix_tb = (ab // 2, tb) if ab > 1 else (ab, tb // 2)
    (r, d, k, v, a, b, g, bonus, q, k_att, v_att, u_last) = _mix_in(x, mods, u_prev0, p, rope,
                                                                   ab=mix_ab, tb=mix_tb)
    pad_t = -t_n % SCAN_CHUNK
    pad = (lambda z: jnp.pad(z, ((0, 0), (0, pad_t), (0, 0)))) if pad_t else (lambda z: z)
    o, s_fin = _chunk_scan(pad(r), pad(d), pad(k), pad(v), pad(a), pad(b), _state_to_blockdiag(s0),
                           nb=CHUNK_SCAN_SEQS)
    o = o[:, :t_n]
    s_fin = _state_from_blockdiag(s_fin)
    if k_buf is None:
        att = _attn(p["attn_sinks"], q, k_att, v_att, k_att, v_att, sb=1, tq=WINDOW, block_prev=True)
        k_new, v_new = k_att[:, -WINDOW:], v_att[:, -WINDOW:]
    else:
        kb = k_buf.reshape(a_n, WINDOW, KV_WIDTH)
        vb = v_buf.reshape(a_n, WINDOW, KV_WIDTH)
        att = _attn(p["attn_sinks"], q, kb, vb, k_att, v_att, sb=attn_sb, tq=t_n, block_prev=False)
        k_new = jnp.concatenate([kb, k_att], axis=1)[:, -WINDOW:]
        v_new = jnp.concatenate([vb, v_att], axis=1)[:, -WINDOW:]
    x = _mix_out(x, mods, o, bonus, g, att, p, ab=ab, tb=tb)
    y = _ffn(x, mods, p["norm_ffn2"], p["ffn2_w_gu"], p["ffn2_w_down"], final_norm_w,
             mod_base=6, ab=ab, tb=tb, final_norm=True)
    kv_shape = (a_n, WINDOW, N_KV_HEADS, HEAD_DIM)
    return y, (s_fin, u_last[:, 0, :], k_new.reshape(kv_shape), v_new.reshape(kv_shape))


def kernel(x_prompt, x_sample, c_prompt, c_sample, state_wkv, state_shift, cache_k, cache_v, w_ada, b_ada,
           norm_ffn1, ffn1_w_gu, ffn1_w_down, norm_mix, w_in, mu_shift, rwkv_w0, rwkv_w2, rwkv_a0, rwkv_a2,
           rwkv_g2, rwkv_k_k, rwkv_k_a, rwkv_r_k, ln_x_w, ln_x_b, attn_sinks, w_out, norm_ffn2, ffn2_w_gu,
           ffn2_w_down, norm_final):
    depth = w_ada.shape[0]
    assert depth == 1, "final norm is fused into the last layer's second FFN"
    bp = x_prompt.shape[0]
    bs = x_sample.shape[0]
    w = RWKV_WIDTH
    lane_head = jnp.arange(w) // HEAD_DIM
    head_ones = (lane_head[:, None] == lane_head[None, :]).astype(BF16)
    fnw = norm_final.reshape(1, 1, D_MODEL)

    n_c = bp + bs
    n_pad = -n_c % 8
    c_all = jnp.concatenate([c_prompt, c_sample, jnp.zeros((n_pad, D_MODEL), F32)], axis=0)

    l = 0
    p = {
        "norm_ffn1": norm_ffn1[l].reshape(1, 1, D_MODEL),
        "ffn1_w_gu": ffn1_w_gu[l].astype(BF16), "ffn1_w_down": ffn1_w_down[l].astype(BF16),
        "norm_mix": norm_mix[l].reshape(1, 1, D_MODEL),
        "w_in": w_in[l].astype(BF16),
        "mu_shift": mu_shift[l].reshape(1, 1, RWKV_COLS),
        "rwkv_w0": rwkv_w0[l].reshape(1, w), "rwkv_w2": rwkv_w2[l].astype(BF16),
        "rwkv_a0": rwkv_a0[l].reshape(1, w), "rwkv_a2": rwkv_a2[l].astype(BF16),
        "rwkv_g2": rwkv_g2[l].astype(BF16),
        "rwkv_k_k": rwkv_k_k[l].reshape(1, w), "rwkv_k_a": rwkv_k_a[l].reshape(1, w),
        "rwkv_r_k": rwkv_r_k[l].reshape(1, w),
        "ln_x_w": ln_x_w[l].reshape(1, w), "ln_x_b": ln_x_b[l].reshape(1, w),
        "attn_sinks": attn_sinks[l],
        "w_out": w_out[l].astype(BF16),
        "norm_ffn2": norm_ffn2[l].reshape(1, 1, D_MODEL),
        "ffn2_w_gu": ffn2_w_gu[l].astype(BF16), "ffn2_w_down": ffn2_w_down[l].astype(BF16),
        "head_ones": head_ones,
        "scan_ones": head_ones[:SCAN_LANES, :SCAN_LANES],
    }
    mods_all = _ada(c_all, w_ada[l], b_ada[l])
    mods_p = mods_all[:bp].reshape(bp, 1, N_ADA * D_MODEL)
    mods_s = mods_all[bp:n_c].reshape(bs, 1, N_ADA * D_MODEL)

    s0_p = jnp.zeros((bp, N_RWKV_HEADS, HEAD_DIM, HEAD_DIM), F32)
    u0_p = jnp.zeros((bp, 1, RWKV_COLS), F32)
    yp, st_p = _layer(x_prompt, mods_p, s0_p, u0_p, None, None, 0, p,
                      ab=1, tb=512, scan_ab=bp, scan_tb=128, attn_sb=1, final_norm_w=fnw)
    t_s = x_sample.shape[1]
    ys, st_s = _layer(x_sample, mods_s, state_wkv[l], state_shift[l].reshape(bs, 1, RWKV_COLS),
                      cache_k[l], cache_v[l], PAST_LEN, p,
                      ab=512 // t_s, tb=t_s, scan_ab=4, scan_tb=t_s, attn_sb=8, final_norm_w=fnw)
    return (yp, ys, st_p[0][None], st_p[1][None], st_p[2][None], st_p[3][None],
            st_s[0][None], st_s[1][None], st_s[2][None], st_s[3][None])
```
